```python
import math
import jax, jax.numpy as jnp
from jax import lax
import numpy as np

D_MODEL = 2048
BATCH = 8
SEQ = 4096
DEPTH = 2

N_MIXERS = 2
ATT_HEADS = 16
ATT_HEAD_DIM = D_MODEL // ATT_HEADS
DILATED_PATTERNS = ((128, 1), (512, 4), (2048, 16))
N_GROUPS = len(DILATED_PATTERNS)
REL_BUCKETS = 32
REL_MAX_DIST = 2048
RET_HEADS = D_MODEL // 256
RET_QK_DIM = D_MODEL
RET_V_DIM = 2 * D_MODEL
RET_HEAD_QK = RET_QK_DIM // RET_HEADS
RET_HEAD_V = RET_V_DIM // RET_HEADS
RET_CHUNK = 128
D_FF = ((8 * D_MODEL + 3 * 256 - 1) // (3 * 256)) * 256
NORM_EPS = 1e-6
MASK_VALUE = -1e30

kernel_name = "hybrid_dilated_attn_retention_block"


def rms_normalize(x):
    x32 = x.astype(jnp.float32)
    return x32 * lax.rsqrt(jnp.mean(x32 * x32, axis=-1, keepdims=True) + NORM_EPS)


def rms_norm(x, gain):
    return (rms_normalize(x) * gain.astype(jnp.float32)).astype(x.dtype)


def t5_bucket(dist):
    max_exact = REL_BUCKETS // 2
    d_f = jnp.maximum(dist, 1).astype(jnp.float32)
    large = max_exact + (jnp.log(d_f / max_exact) / math.log(REL_MAX_DIST / max_exact)
                         * (REL_BUCKETS - max_exact)).astype(jnp.int32)
    large = jnp.minimum(large, REL_BUCKETS - 1)
    return jnp.where(dist < max_exact, dist, large)


def dilated_group(q, k, v, window, dil, bias_table):
    B, S, H, Dh = q.shape
    steps = window // dil
    blk = steps
    span = dil * blk
    s_pad = -(-S // span) * span
    L = s_pad // dil
    nb = L // blk

    def to_blocks(t):
        t = jnp.pad(t, ((0, 0), (0, s_pad - S), (0, 0), (0, 0)))
        t = t.reshape(B, L, dil, H, Dh).transpose(0, 2, 1, 3, 4)
        return t.reshape(B, dil, nb, blk, H, Dh)

    def with_prev(t):
        prev = jnp.pad(t[:, :, :-1], ((0, 0), (0, 0), (1, 0), (0, 0), (0, 0), (0, 0)))
        return jnp.concatenate([prev, t], axis=3)

    qb = to_blocks(q)
    kk = with_prev(to_blocks(k))
    vv = with_prev(to_blocks(v))

    s = jnp.einsum("brnqhd,brnkhd->brnhqk", qb, kk).astype(jnp.float32) * (Dh ** -0.5)
    qi = jnp.arange(blk)[:, None]
    ki = jnp.arange(2 * blk)[None, :]
    delta = blk + qi - ki
    bucket = t5_bucket(jnp.maximum(delta, 0) * dil)
    bias = bias_table[bucket].transpose(2, 0, 1).astype(jnp.float32)
    valid = (delta >= 0) & (delta <= steps)
    first = (jnp.arange(nb) == 0)[:, None, None]
    valid = valid[None] & ~(first & (ki < blk)[None])
    s = jnp.where(valid[:, None], s + bias, MASK_VALUE)
    lse = jax.nn.logsumexp(s, axis=-1)
    p = jnp.exp(s - lse[..., None]).astype(v.dtype)
    o = jnp.einsum("brnhqk,brnkhd->brnqhd", p, vv)

    o = o.reshape(B, dil, L, H, Dh).transpose(0, 2, 1, 3, 4).reshape(B, s_pad, H, Dh)[:, :S]
    lse = lse.transpose(0, 1, 2, 4, 3).reshape(B, dil, L, H).transpose(0, 2, 1, 3)
    lse = lse.reshape(B, s_pad, H)[:, :S]
    return o, lse


def dilated_attention(h, w_qkv, q_gain, k_gain, w_o, rel_bias):
    B, S, _ = h.shape
    qkv = (h @ w_qkv).reshape(B, S, N_GROUPS, 3, ATT_HEADS, ATT_HEAD_DIM)
    outs, lses = [], []
    for g, (window, dil) in enumerate(DILATED_PATTERNS):
        q = rms_norm(qkv[:, :, g, 0], q_gain[g])
        k = rms_norm(qkv[:, :, g, 1], k_gain[g])
        v = qkv[:, :, g, 2]
        o, lse = dilated_group(q, k, v, window, dil,
                               rel_bias[:, g * ATT_HEADS:(g + 1) * ATT_HEADS])
        outs.append(o)
        lses.append(lse)
    weights = jax.nn.softmax(jnp.stack(lses, axis=-1), axis=-1)
    o = jnp.einsum("gbshd,bshg->bshd", jnp.stack(outs, axis=0), weights.astype(h.dtype))
    return o.reshape(B, S, D_MODEL) @ w_o


def retention_rotary(t):
    S = t.shape[1]
    half = t.shape[-1] // 2
    inv = 1.0 / (10000.0 ** jnp.linspace(0.0, 1.0, half, dtype=jnp.float32))
    ang = jnp.arange(S, dtype=jnp.float32)[:, None] * inv[None, :]
    cos = jnp.cos(ang)[None, :, None, :]
    sin = jnp.sin(ang)[None, :, None, :]
    t32 = t.astype(jnp.float32)
    t1, t2 = t32[..., :half], t32[..., half:]
    return jnp.concatenate([t1 * cos - t2 * sin, t1 * sin + t2 * cos], axis=-1)


def retention(h, w_qkvg, w_o):
    B, S, _ = h.shape
    q, k, v, g = jnp.split(h @ w_qkvg,
                           [RET_QK_DIM, 2 * RET_QK_DIM, 2 * RET_QK_DIM + RET_V_DIM], axis=-1)
    q = retention_rotary(q.reshape(B, S, RET_HEADS, RET_HEAD_QK))
    k = retention_rotary(k.reshape(B, S, RET_HEADS, RET_HEAD_QK)) * (RET_HEAD_QK ** -0.5)
    v = v.reshape(B, S, RET_HEADS, RET_HEAD_V).astype(jnp.float32)

    C = RET_CHUNK
    nc = S // C
    log_gamma = jnp.log(1.0 - 2.0 ** (-5.0 - jnp.arange(RET_HEADS, dtype=jnp.float32)))
    pos = jnp.arange(C, dtype=jnp.float32)
    diff = pos[:, None] - pos[None, :]
    inner_decay = jnp.where(diff[None] >= 0,
                            jnp.exp(jnp.maximum(diff, 0.0)[None] * log_gamma[:, None, None]),
                            0.0)
    cross_decay = jnp.exp((pos[:, None] + 1.0) * log_gamma[None, :])
    state_decay = jnp.exp((C - 1.0 - pos)[:, None] * log_gamma[None, :])
    chunk_decay = jnp.exp(C * log_gamma)

    def chunks(t):
        return jnp.moveaxis(t.reshape(B, nc, C, RET_HEADS, t.shape[-1]), 1, 0)

    def step(state, qkv_c):
        qc, kc, vc = qkv_c
        scores = jnp.einsum("bnhk,bmhk->bhnm", qc, kc) * inner_decay
        o = (jnp.einsum("bhnm,bmhv->bnhv", scores, vc)
             + jnp.einsum("bnhk,bhkv->bnhv", qc, state) * cross_decay[None, :, :, None])
        state = (state * chunk_decay[None, :, None, None]
                 + jnp.einsum("bmhk,bmhv->bhkv", kc * state_decay[None, :, :, None], vc))
        return state, o

    state0 = jnp.zeros((B, RET_HEADS, RET_HEAD_QK, RET_HEAD_V), jnp.float32)
    _, o = lax.scan(step, state0, (chunks(q), chunks(k), chunks(v)))
    o = jnp.moveaxis(o, 0, 1).reshape(B, S, RET_HEADS, RET_HEAD_V)
    o = rms_normalize(o).reshape(B, S, RET_V_DIM).astype(h.dtype)
    return (jax.nn.silu(g) * o) @ w_o


def swiglu(h, w_up, w_down):
    gate, up = jnp.split(h @ w_up, 2, axis=-1)
    return (jax.nn.silu(gate) * up) @ w_down


def setup_inputs(seed: int = 0) -> dict:
    key = jax.random.key(seed)
    ks = jax.random.split(key, 16)
    n_att = (DEPTH + N_MIXERS - 1) // N_MIXERS
    n_ret = DEPTH // N_MIXERS

    def nrm(k, shape, scale):
        return jax.random.normal(k, shape, jnp.float32) * scale

    return {
        "x": nrm(ks[0], (BATCH, SEQ, D_MODEL), 1.0),
        "c": nrm(ks[1], (BATCH, D_MODEL), 1.0),
        "w_mod": nrm(ks[2], (DEPTH, D_MODEL, 6 * D_MODEL), 0.5 * D_MODEL ** -0.5),
        "b_mod": nrm(ks[3], (DEPTH, 6 * D_MODEL), 0.02),
        "norm_mix": 1.0 + nrm(ks[4], (DEPTH, D_MODEL), 0.02),
        "norm_ffn": 1.0 + nrm(ks[5], (DEPTH, D_MODEL), 0.02),
        "rel_bias": nrm(ks[6], (REL_BUCKETS, N_GROUPS * ATT_HEADS), 0.5),
        "att_w_qkv": nrm(ks[7], (n_att, D_MODEL, N_GROUPS * 3 * ATT_HEADS * ATT_HEAD_DIM),
                          D_MODEL ** -0.5),
        "att_q_gain": 1.0 + nrm(ks[8], (n_att, N_GROUPS, ATT_HEAD_DIM), 0.02),
        "att_k_gain": 1.0 + nrm(ks[9], (n_att, N_GROUPS, ATT_HEAD_DIM), 0.02),
        "att_w_o": nrm(ks[10], (n_att, ATT_HEADS * ATT_HEAD_DIM, D_MODEL), D_MODEL ** -0.5),
        "ret_w_qkvg": nrm(ks[11], (n_ret, D_MODEL, 2 * RET_QK_DIM + 2 * RET_V_DIM),
                           D_MODEL ** -0.5),
        "ret_w_o": nrm(ks[12], (n_ret, RET_V_DIM, D_MODEL), RET_V_DIM ** -0.5),
        "ffn_w_up": nrm(ks[13], (DEPTH, D_MODEL, 2 * D_FF), D_MODEL ** -0.5),
        "ffn_w_down": nrm(ks[14], (DEPTH, D_FF, D_MODEL), D_FF ** -0.5),
    }


def reference(x, c, w_mod, b_mod, norm_mix, norm_ffn, rel_bias, att_w_qkv, att_q_gain,
              att_k_gain, att_w_o, ret_w_qkvg, ret_w_o, ffn_w_up, ffn_w_down):
    cond = jax.nn.silu(c)
    for i in range(DEPTH):
        mod = (cond @ w_mod[i] + b_mod[i])[:, None, :]
        sh1, sc1, g1, sh2, sc2, g2 = jnp.split(mod, 6, axis=-1)
        j = i // N_MIXERS
        h = rms_norm(x, norm_mix[i]) * (1.0 + sc1) + sh1
        if i % N_MIXERS == 0:
            y = dilated_attention(h, att_w_qkv[j], att_q_gain[j], att_k_gain[j], att_w_o[j],
                                  rel_bias)
        else:
            y = retention(h, ret_w_qkvg[j], ret_w_o[j])
        x = x + g1 * y
        h = rms_norm(x, norm_ffn[i]) * (1.0 + sc2) + sh2
        x = x + g2 * swiglu(h, ffn_w_up[i], ffn_w_down[i])
    return x
```

```python
import functools
import math

import numpy as np
import jax
import jax.numpy as jnp
from jax import lax
from jax.experimental import pallas as pl
from jax.experimental.pallas import tpu as pltpu

F32 = jnp.float32
BF16 = jnp.bfloat16

NORM_EPS = 1e-6
MASK_VALUE = -1e30

ATT_HEADS = 16
HEAD_DIM = 128
DILATED_PATTERNS = ((128, 1), (512, 4), (2048, 16))
N_GROUPS = len(DILATED_PATTERNS)
ATT_BLK = 128
REL_BUCKETS = 32
REL_MAX_DIST = 2048

RET_HEADS = 8
RET_HEAD_QK = 256
RET_HEAD_V = 512
RET_CHUNK = 128

VMEM_LIMIT_BYTES = 56 * 1024 * 1024

TM = 1024
TN = 1024
ROW_CHUNK = 256


def _params(n_axes):
    return pltpu.CompilerParams(dimension_semantics=("arbitrary",) * n_axes,
                                vmem_limit_bytes=VMEM_LIMIT_BYTES)


def _silu(v):
    return v / (1.0 + jnp.exp(-v))


def _mod_kernel(c_ref, w_ref, b_ref, o_ref):
    cond = _silu(c_ref[...]).astype(BF16)
    o_ref[...] = (jnp.dot(cond, w_ref[...].astype(BF16), preferred_element_type=F32)
                  + b_ref[...])


def _modulation(c, w_mod, b_mod):
    depth, d, n = w_mod.shape
    b = c.shape[0]
    tn = 1024
    return pl.pallas_call(
        _mod_kernel,
        grid=(depth, n // tn),
        in_specs=[pl.BlockSpec((b, d), lambda l, j: (0, 0)),
                  pl.BlockSpec((None, d, tn), lambda l, j: (l, 0, j)),
                  pl.BlockSpec((None, 1, tn), lambda l, j: (l, 0, j))],
        out_specs=pl.BlockSpec((None, b, tn), lambda l, j: (l, 0, j)),
        out_shape=jax.ShapeDtypeStruct((depth, b, n), F32),
        compiler_params=_params(2),
        name="adaln_mod",
    )(c, w_mod, b_mod.reshape(depth, 1, n))


def _norm_mod_to_scratch(x_ref, gain_ref, sc_ref, sh_ref, h_ref):
    gain = gain_ref[...]
    scale = 1.0 + sc_ref[...]
    shift = sh_ref[...]

    def body(r, carry):
        rows = pl.ds(pl.multiple_of(r * ROW_CHUNK, ROW_CHUNK), ROW_CHUNK)
        xx = x_ref[rows, :]
        ms = jnp.mean(xx * xx, axis=-1, keepdims=True)
        normed = xx * lax.rsqrt(ms + NORM_EPS) * gain
        h_ref[rows, :] = (normed * scale + shift).astype(BF16)
        return carry

    lax.fori_loop(0, x_ref.shape[0] // ROW_CHUNK, body, 0)


def _mod_spec(d, tiles_per_batch, row, chunk):
    return pl.BlockSpec((None, 1, d), lambda i, j: (row + i // tiles_per_batch, 0, chunk))


def _att_proj_kernel(x_ref, gain_ref, sc_ref, sh_ref, w_ref, hg_ref, o_ref, h_ref, *, d_model):
    j = pl.program_id(1)

    @pl.when(j == 0)
    def _():
        _norm_mod_to_scratch(x_ref, gain_ref, sc_ref, sh_ref, h_ref)

    acc = jnp.dot(h_ref[...], w_ref[...], preferred_element_type=F32)
    tn = acc.shape[1]
    is_v = ((j * tn) // d_model) % 3 == 2

    @pl.when(is_v)
    def _():
        o_ref[...] = acc.astype(BF16)

    @pl.when(jnp.logical_not(is_v))
    def _():
        for hh in range(tn // HEAD_DIM):
            sl = slice(hh * HEAD_DIM, (hh + 1) * HEAD_DIM)
            a = acc[:, sl]
            ms = jnp.mean(a * a, axis=-1, keepdims=True)
            o_ref[:, sl] = (a * lax.rsqrt(ms + NORM_EPS) * hg_ref[:, sl]).astype(BF16)


def _att_proj(x, gain, mod3, mod_row, w, head_gain, tiles_per_batch):
    t, d = x.shape
    n = w.shape[1]
    return pl.pallas_call(
        functools.partial(_att_proj_kernel, d_model=d),
        grid=(t // TM, n // TN),
        in_specs=[pl.BlockSpec((TM, d), lambda i, j: (i, 0)),
                  pl.BlockSpec((1, d), lambda i, j: (0, 0)),
                  _mod_spec(d, tiles_per_batch, mod_row, 1),
                  _mod_spec(d, tiles_per_batch, mod_row, 0),
                  pl.BlockSpec((d, TN), lambda i, j: (0, j)),
                  pl.BlockSpec((1, TN), lambda i, j: (0, j))],
        out_specs=pl.BlockSpec((TM, TN), lambda i, j: (i, j)),
        out_shape=jax.ShapeDtypeStruct((t, n), BF16),
        scratch_shapes=[pltpu.VMEM((TM, d), BF16)],
        compiler_params=_params(2),
        name="att_qkv_proj",
    )(x, gain, mod3, mod3, w, head_gain)


def _proj_kernel(x_ref, gain_ref, sc_ref, sh_ref, w_ref, o_ref, h_ref):
    @pl.when(pl.program_id(1) == 0)
    def _():
        _norm_mod_to_scratch(x_ref, gain_ref, sc_ref, sh_ref, h_ref)

    o_ref[...] = jnp.dot(h_ref[...], w_ref[...], preferred_element_type=F32).astype(BF16)


def _proj(x, gain, mod3, mod_row, w, tiles_per_batch):
    t, d = x.shape
    n = w.shape[1]
    return pl.pallas_call(
        _proj_kernel,
        grid=(t // TM, n // TN),
        in_specs=[pl.BlockSpec((TM, d), lambda i, j: (i, 0)),
                  pl.BlockSpec((1, d), lambda i, j: (0, 0)),
                  _mod_spec(d, tiles_per_batch, mod_row, 1),
                  _mod_spec(d, tiles_per_batch, mod_row, 0),
                  pl.BlockSpec((d, TN), lambda i, j: (0, j))],
        out_specs=pl.BlockSpec((TM, TN), lambda i, j: (i, j)),
        out_shape=jax.ShapeDtypeStruct((t, n), BF16),
        scratch_shapes=[pltpu.VMEM((TM, d), BF16)],
        compiler_params=_params(2),
        name="ret_qkvg_proj",
    )(x, gain, mod3, mod3, w)


def _ffn_up_kernel(x_ref, gain_ref, sc_ref, sh_ref, wg_ref, wu_ref, o_ref, h_ref):
    @pl.when(pl.program_id(1) == 0)
    def _():
        _norm_mod_to_scratch(x_ref, gain_ref, sc_ref, sh_ref, h_ref)

    h = h_ref[...]
    gate = jnp.dot(h, wg_ref[...], preferred_element_type=F32)
    up = jnp.dot(h, wu_ref[...], preferred_element_type=F32)
    o_ref[...] = (_silu(gate) * up).astype(BF16)


def _ffn_up(x, gain, mod3, mod_row, w_up, tiles_per_batch):
    t, d = x.shape
    d_ff = w_up.shape[1] // 2
    tf = 512
    n_f = d_ff // tf
    return pl.pallas_call(
        _ffn_up_kernel,
        grid=(t // TM, n_f),
        in_specs=[pl.BlockSpec((TM, d), lambda i, j: (i, 0)),
                  pl.BlockSpec((1, d), lambda i, j: (0, 0)),
                  _mod_spec(d, tiles_per_batch, mod_row, 4),
                  _mod_spec(d, tiles_per_batch, mod_row, 3),
                  pl.BlockSpec((d, tf), lambda i, j: (0, j)),
                  pl.BlockSpec((d, tf), lambda i, j: (0, j + n_f))],
        out_specs=pl.BlockSpec((TM, tf), lambda i, j: (i, j)),
        out_shape=jax.ShapeDtypeStruct((t, d_ff), BF16),
        scratch_shapes=[pltpu.VMEM((TM, d), BF16)],
        compiler_params=_params(2),
        name="ffn_up",
    )(x, gain, mod3, mod3, w_up, w_up)


def _out_proj_kernel(a_ref, w_ref, x_ref, g_ref, o_ref):
    y = jnp.dot(a_ref[...], w_ref[...], preferred_element_type=F32)
    o_ref[...] = x_ref[...] + g_ref[...] * y


def _out_proj(a, w, x, mod3, mod_row, gate_chunk, tiles_per_batch, tm, tn, name):
    t, k = a.shape
    n = w.shape[1]
    return pl.pallas_call(
        _out_proj_kernel,
        grid=(t // tm, n // tn),
        in_specs=[pl.BlockSpec((tm, k), lambda i, j: (i, 0)),
                  pl.BlockSpec((k, tn), lambda i, j: (0, j)),
                  pl.BlockSpec((tm, tn), lambda i, j: (i, j)),
                  pl.BlockSpec((None, 1, tn),
                               lambda i, j: (mod_row + i // tiles_per_batch, 0,
                                             gate_chunk * (n // tn) + j))],
        out_specs=pl.BlockSpec((tm, tn), lambda i, j: (i, j)),
        out_shape=jax.ShapeDtypeStruct((t, n), F32),
        compiler_params=_params(2),
        name=name,
    )(a, w, x, mod3)


def _bias_kernel(tab_ref, bucket_ref, valid_ref, o_ref):
    g = pl.program_id(0)
    h = pl.program_id(1)
    bucket = bucket_ref[...]
    col = g * ATT_HEADS + h
    acc = jnp.zeros(bucket.shape, F32)
    for b in range(REL_BUCKETS):
        acc = jnp.where(bucket == b, tab_ref[b, col], acc)
    for first in range(2):
        o_ref[first] = jnp.where(valid_ref[first] != 0, acc, MASK_VALUE)


def _t5_bucket_np(dist):
    max_exact = REL_BUCKETS // 2
    d_f = np.maximum(dist, 1).astype(np.float32)
    large = max_exact + (np.log(d_f / np.float32(max_exact))
                         / np.float32(math.log(REL_MAX_DIST / max_exact))
                         * np.float32(REL_BUCKETS - max_exact)).astype(np.int32)
    large = np.minimum(large, REL_BUCKETS - 1)
    return np.where(dist < max_exact, dist, large).astype(np.int32)


def _bias_tiles(rel_bias):
    blk = ATT_BLK
    qi = np.arange(blk)[:, None]
    ki = np.arange(2 * blk)[None, :]
    delta = blk + qi - ki
    band = (delta >= 0) & (delta <= blk)
    valid = np.stack([band & (ki >= blk), band]).astype(np.int32)
    buckets = np.stack([_t5_bucket_np(np.maximum(delta, 0) * dil)
                        for _, dil in DILATED_PATTERNS])
    return pl.pallas_call(
        _bias_kernel,
        grid=(N_GROUPS, ATT_HEADS),
        in_specs=[pl.BlockSpec(memory_space=pltpu.SMEM),
                  pl.BlockSpec((None, blk, 2 * blk), lambda g, h: (g, 0, 0)),
                  pl.BlockSpec((2, blk, 2 * blk), lambda g, h: (0, 0, 0))],
        out_specs=pl.BlockSpec((None, 2, None, blk, 2 * blk), lambda g, h: (g, 0, h, 0, 0)),
        out_shape=jax.ShapeDtypeStruct((N_GROUPS, 2, ATT_HEADS, blk, 2 * blk), F32),
        compiler_params=_params(2),
        name="rel_bias_tiles",
    )(rel_bias, jnp.asarray(buckets), jnp.asarray(valid))


def _att_kernel(*refs, has_prev, write_lse):
    q_ref, kc_ref, kp_ref, vc_ref, vp_ref, bias_ref = refs[:6]
    pos = 6
    if has_prev:
        op_ref, lp_ref = refs[pos:pos + 2]
        pos += 2
    o_ref = refs[pos]
    lse_ref = refs[pos + 1] if write_lse else None

    n = pl.program_id(2)
    variant = jnp.where(n == 0, 0, 1)
    blk = ATT_BLK
    dn = (((1,), (1,)), ((), ()))
    lane = lax.broadcasted_iota(jnp.int32, (blk, HEAD_DIM), 1)
    lse_tile = jnp.zeros((blk, HEAD_DIM), F32)
    if has_prev:
        lp_tile = lp_ref[...]

    for h in range(ATT_HEADS):
        sl = slice(h * HEAD_DIM, (h + 1) * HEAD_DIM)
        q = q_ref[:, sl]
        bias = bias_ref[variant, h]
        s_p = lax.dot_general(q, kp_ref[:, sl], dn, preferred_element_type=F32) + bias[:, :blk]
        s_c = lax.dot_general(q, kc_ref[:, sl], dn, preferred_element_type=F32) + bias[:, blk:]
        m = jnp.maximum(jnp.max(s_p, axis=-1, keepdims=True),
                        jnp.max(s_c, axis=-1, keepdims=True))
        p_p = jnp.exp(s_p - m)
        p_c = jnp.exp(s_c - m)
        l = jnp.sum(p_p, axis=-1, keepdims=True) + jnp.sum(p_c, axis=-1, keepdims=True)
        o = (jnp.dot(p_p.astype(BF16), vp_ref[:, sl], preferred_element_type=F32)
             + jnp.dot(p_c.astype(BF16), vc_ref[:, sl], preferred_element_type=F32)) / l
        lse = m + jnp.log(l)
        if has_prev:
            lse_prev = jnp.sum(jnp.where(lane == h, lp_tile, 0.0), axis=-1, keepdims=True)
            mx = jnp.maximum(lse_prev, lse)
            w_prev = jnp.exp(lse_prev - mx)
            w_cur = jnp.exp(lse - mx)
            den = w_prev + w_cur
            o = (w_prev * op_ref[:, sl].astype(F32) + w_cur * o) / den
            lse = mx + jnp.log(den)
        o_ref[:, sl] = o.astype(BF16)
        if write_lse:
            lse_tile = jnp.where(lane == h, lse, lse_tile)

    if write_lse:
        lse_ref[...] = lse_tile


def _attention_group(qkv, bias, g, prev, batch, seq, write_lse):
    _, dil = DILATED_PATTERNS[g]
    blk = ATT_BLK
    width = ATT_HEADS * HEAD_DIM
    n_cols = qkv.shape[1]
    l_len = seq // dil
    nb = l_len // blk
    cpr = n_cols // width

    qkv_v = qkv.reshape(batch, l_len, dil * n_cols)

    def col_spec(which, shift):
        return pl.BlockSpec(
            (None, blk, width),
            lambda b, r, n: (b, jnp.maximum(n - shift, 0), r * cpr + g * 3 + which))

    in_specs = [col_spec(0, 0), col_spec(1, 0), col_spec(1, 1), col_spec(2, 0), col_spec(2, 1),
                pl.BlockSpec((None, 2, ATT_HEADS, blk, 2 * blk), lambda b, r, n: (g, 0, 0, 0, 0))]
    args = [qkv_v, qkv_v, qkv_v, qkv_v, qkv_v, bias]
    o_spec = pl.BlockSpec((None, blk, width), lambda b, r, n: (b, n, r))
    lse_spec = pl.BlockSpec((None, blk, HEAD_DIM), lambda b, r, n: (b, n, r))
    if prev is not None:
        in_specs += [o_spec, lse_spec]
        args += [prev[0].reshape(batch, l_len, dil * width),
                 prev[1].reshape(batch, l_len, dil * HEAD_DIM)]

    out_specs = [o_spec]
    out_shape = [jax.ShapeDtypeStruct((batch, l_len, dil * width), BF16)]
    if write_lse:
        out_specs.append(lse_spec)
        out_shape.append(jax.ShapeDtypeStruct((batch, l_len, dil * HEAD_DIM), F32))

    outs = pl.pallas_call(
        functools.partial(_att_kernel, has_prev=prev is not None, write_lse=write_lse),
        grid=(batch, dil, nb),
        in_specs=in_specs,
        out_specs=out_specs,
        out_shape=out_shape,
        compiler_params=_params(3),
        name=f"dilated_attention_g{g}",
    )(*args)
    o = outs[0].reshape(batch * seq, width)
    lse = outs[1].reshape(batch * seq, HEAD_DIM) if write_lse else None
    return o, lse


RET_TOKENS = 512


def _retention_kernel(q_ref, k_ref, v_ref, g_ref, cos_ref, sin_ref, inner_ref, cross_ref,
                      sdec_ref, cdec_ref, o_ref, state_ref):
    @pl.when(pl.program_id(2) == 0)
    def _():
        state_ref[...] = jnp.zeros_like(state_ref)

    half = RET_HEAD_QK // 2
    dn_nt = (((1,), (1,)), ((), ()))
    dn_tn = (((0,), (0,)), ((), ()))
    inner = inner_ref[...]
    cross = cross_ref[...]
    sdec = sdec_ref[...]
    cdec = cdec_ref[...]
    k_scale = RET_HEAD_QK ** -0.5

    for c in range(RET_TOKENS // RET_CHUNK):
        rows = slice(c * RET_CHUNK, (c + 1) * RET_CHUNK)
        cos = cos_ref[rows, :]
        sin = sin_ref[rows, :]
        q = q_ref[rows, :].astype(F32)
        k = k_ref[rows, :].astype(F32)
        q1, q2 = q[:, :half], q[:, half:]
        k1, k2 = k[:, :half], k[:, half:]
        qr1 = q1 * cos - q2 * sin
        qr2 = q1 * sin + q2 * cos
        kr1 = (k1 * cos - k2 * sin) * k_scale
        kr2 = (k1 * sin + k2 * cos) * k_scale
        q_b = jnp.concatenate([qr1, qr2], axis=1).astype(BF16)
        qc_b = jnp.concatenate([qr1 * cross, qr2 * cross], axis=1).astype(BF16)
        k_b = jnp.concatenate([kr1, kr2], axis=1).astype(BF16)
        ks_b = jnp.concatenate([kr1 * sdec, kr2 * sdec], axis=1).astype(BF16)
        v = v_ref[rows, :]

        scores = lax.dot_general(q_b, k_b, dn_nt, preferred_element_type=F32) * inner
        state = state_ref[...]
        o = (jnp.dot(scores.astype(BF16), v, preferred_element_type=F32)
             + jnp.dot(qc_b, state.astype(BF16), preferred_element_type=F32))
        state_ref[...] = state * cdec + lax.dot_general(ks_b, v, dn_tn,
                                                        preferred_element_type=F32)

        ms = jnp.mean(o * o, axis=-1, keepdims=True)
        gate = g_ref[rows, :].astype(F32)
        o_ref[rows, :] = (_silu(gate) * (o * lax.rsqrt(ms + NORM_EPS))).astype(BF16)


def _retention(qkvg, batch, seq):
    c = RET_CHUNK
    hq, hv = RET_HEAD_QK, RET_HEAD_V
    n_qk = RET_HEADS * hq
    n_v = RET_HEADS * hv

    half = hq // 2
    inv = 1.0 / (10000.0 ** jnp.linspace(0.0, 1.0, half, dtype=F32))
    ang = jnp.arange(seq, dtype=F32)[:, None] * inv[None, :]
    cos, sin = jnp.cos(ang), jnp.sin(ang)
    log_gamma = jnp.log(1.0 - 2.0 ** (-5.0 - jnp.arange(RET_HEADS, dtype=F32)))
    pos = jnp.arange(c, dtype=F32)
    diff = pos[:, None] - pos[None, :]
    inner = jnp.where(diff[None] >= 0,
                      jnp.exp(jnp.maximum(diff, 0.0)[None] * log_gamma[:, None, None]), 0.0)
    cross = jnp.exp((pos[None, :] + 1.0) * log_gamma[:, None])
    sdec = jnp.exp((c - 1.0 - pos)[None, :] * log_gamma[:, None])
    cdec = jnp.exp(c * log_gamma)
    cross_l = jnp.broadcast_to(cross[:, :, None], (RET_HEADS, c, half))
    sdec_l = jnp.broadcast_to(sdec[:, :, None], (RET_HEADS, c, half))
    cdec_l = jnp.broadcast_to(cdec[:, None, None], (RET_HEADS, 1, hv))

    x = qkvg.reshape(batch, seq, qkvg.shape[1])
    tt = RET_TOKENS
    head_tab = lambda shape: pl.BlockSpec((None,) + shape, lambda b, h, s: (h, 0, 0))
    out = pl.pallas_call(
        _retention_kernel,
        grid=(batch, RET_HEADS, seq // tt),
        in_specs=[pl.BlockSpec((None, tt, hq), lambda b, h, s: (b, s, h)),
                  pl.BlockSpec((None, tt, hq), lambda b, h, s: (b, s, n_qk // hq + h)),
                  pl.BlockSpec((None, tt, hv), lambda b, h, s: (b, s, 2 * n_qk // hv + h)),
                  pl.BlockSpec((None, tt, hv), lambda b, h, s: (b, s, (2 * n_qk + n_v) // hv + h)),
                  pl.BlockSpec((tt, half), lambda b, h, s: (s, 0)),
                  pl.BlockSpec((tt, half), lambda b, h, s: (s, 0)),
                  head_tab((c, c)), head_tab((c, half)), head_tab((c, half)), head_tab((1, hv))],
        out_specs=pl.BlockSpec((None, tt, hv), lambda b, h, s: (b, s, h)),
        out_shape=jax.ShapeDtypeStruct((batch, seq, n_v), BF16),
        scratch_shapes=[pltpu.VMEM((hq, hv), F32)],
        compiler_params=_params(3),
        name="retention",
    )(x, x, x, x, cos, sin, inner, cross_l, sdec_l, cdec_l)
    return out.reshape(batch * seq, n_v)


def kernel(x, c, w_mod, b_mod, norm_mix, norm_ffn, rel_bias, att_w_qkv, att_q_gain, att_k_gain,
           att_w_o, ret_w_qkvg, ret_w_o, ffn_w_up, ffn_w_down):
    batch, seq, d = x.shape
    depth = w_mod.shape[0]
    assert seq % TM == 0 and d == ATT_HEADS * HEAD_DIM
    tiles_per_batch = seq // TM

    mod = _modulation(c, w_mod, b_mod)
    mod3 = mod.reshape(depth * batch, 1, 6 * d)
    bias = _bias_tiles(rel_bias)
    xf = x.reshape(batch * seq, d)

    for i in range(depth):
        jdx = i // 2
        mod_row = i * batch
        gain_mix = norm_mix[i].reshape(1, d)
        if i % 2 == 0:
            scale = HEAD_DIM ** -0.5
            ones = jnp.ones((ATT_HEADS * HEAD_DIM,), F32)
            head_gain = jnp.concatenate(
                [jnp.concatenate([jnp.tile(att_q_gain[jdx, g] * scale, ATT_HEADS),
                                  jnp.tile(att_k_gain[jdx, g], ATT_HEADS), ones])
                 for g in range(N_GROUPS)]).reshape(1, -1)
            qkv = _att_proj(xf, gain_mix, mod3, mod_row, att_w_qkv[jdx].astype(BF16), head_gain,
                            tiles_per_batch)
            prev = None
            for g in range(N_GROUPS):
                last = g == N_GROUPS - 1
                o, lse = _attention_group(qkv, bias, g, prev, batch, seq, write_lse=not last)
                prev = (o, lse)
            mixed, w_out = prev[0], att_w_o[jdx].astype(BF16)
            name = "att_out_proj"
        else:
            qkvg = _proj(xf, gain_mix, mod3, mod_row, ret_w_qkvg[jdx].astype(BF16),
                         tiles_per_batch)
            mixed, w_out = _retention(qkvg, batch, seq), ret_w_o[jdx].astype(BF16)
            name = "ret_out_proj"
        xf = _out_proj(mixed, w_out, xf, mod3, mod_row, 2, tiles_per_batch, TM, TN, name)

        act = _ffn_up(xf, norm_ffn[i].reshape(1, d), mod3, mod_row, ffn_w_up[i].astype(BF16),
                      tiles_per_batch)
        xf = _out_proj(act, ffn_w_down[i].astype(BF16), xf, mod3, mod_row, 5, tiles_per_batch,
                       TM, 512, "ffn_down")
    return xf.reshape(batch, seq, d)
```

```python
import functools
import math

import numpy as np
import jax
import jax.numpy as jnp
from jax import lax
from jax.experimental import pallas as pl
from jax.experimental.pallas import tpu as pltpu

F32 = jnp.float32
BF16 = jnp.bfloat16

NORM_EPS = 1e-6
MASK_VALUE = -1e30
LANES = 128

ATT_HEADS = 16
HEAD_DIM = 128
DILATED_PATTERNS = ((128, 1), (512, 4), (2048, 16))
N_GROUPS = len(DILATED_PATTERNS)
ATT_BLK = 128
REL_BUCKETS = 32
REL_MAX_DIST = 2048

RET_HEADS = 8
RET_HEAD_QK = 256
RET_HEAD_V = 512
RET_CHUNK = 128

VMEM_LIMIT_BYTES = 56 * 1024 * 1024

TM = 1024
TN = 1024
ROW_CHUNK = 256
STAGE_COLS = 4


def _params(n_axes):
    return pltpu.CompilerParams(dimension_semantics=("arbitrary",) * n_axes,
                                vmem_limit_bytes=VMEM_LIMIT_BYTES)


def _silu(v):
    return v / (1.0 + jnp.exp(-v))


def _row_chunk(i, size):
    if isinstance(i, int):
        return pl.ds(i * size, size)
    return pl.ds(pl.multiple_of(i * size, size), size)


def _mod_kernel(c_ref, w_ref, b_ref, o_ref):
    cond = _silu(c_ref[...]).astype(BF16)
    o_ref[...] = (jnp.dot(cond, w_ref[...].astype(BF16), preferred_element_type=F32)
                  + b_ref[...])


def _modulation(c, w_mod, b_mod):
    depth, d, n = w_mod.shape
    b = c.shape[0]
    tn = 1024
    return pl.pallas_call(
        _mod_kernel,
        grid=(depth, n // tn),
        in_specs=[pl.BlockSpec((b, d), lambda l, j: (0, 0)),
                  pl.BlockSpec((None, d, tn), lambda l, j: (l, 0, j)),
                  pl.BlockSpec((None, 1, tn), lambda l, j: (l, 0, j))],
        out_specs=pl.BlockSpec((None, b, tn), lambda l, j: (l, 0, j)),
        out_shape=jax.ShapeDtypeStruct((depth, b, n), F32),
        compiler_params=_params(2),
        name="adaln_mod",
    )(c, w_mod, b_mod.reshape(depth, 1, n))


def _norm_mod_to_scratch(x_ref, gain_ref, sc_ref, sh_ref, h_ref):
    gain = gain_ref[...]
    scale = 1.0 + sc_ref[...]
    shift = sh_ref[...]

    def body(r, carry):
        rows = _row_chunk(r, ROW_CHUNK)
        xx = x_ref[rows, :]
        ms = jnp.mean(xx * xx, axis=-1, keepdims=True)
        normed = xx * lax.rsqrt(ms + NORM_EPS) * gain
        h_ref[rows, :] = (normed * scale + shift).astype(BF16)
        return carry

    lax.fori_loop(0, x_ref.shape[0] // ROW_CHUNK, body, 0)


def _mod_spec(d, tiles_per_batch, row, chunk):
    return pl.BlockSpec((None, 1, d), lambda i, j: (row + i // tiles_per_batch, 0, chunk))


def _att_prologue(x_ref, gain_ref, sc_ref, sh_ref, h_ref, rs_ref, stage_ref):
    tm, d = x_ref.shape
    n_chunks = tm // ROW_CHUNK

    def row_scale(i, carry):
        rows = _row_chunk(i, ROW_CHUNK)
        xx = x_ref[rows, :]
        ms = jnp.mean(xx * xx, axis=-1, keepdims=True)
        rs_ref[rows, :] = jnp.broadcast_to(lax.rsqrt(ms + NORM_EPS), (ROW_CHUNK, LANES))
        return carry

    lax.fori_loop(0, n_chunks, row_scale, 0)

    for cg in range(d // (STAGE_COLS * LANES)):
        col = lambda cc: slice((cg * STAGE_COLS + cc) * LANES, (cg * STAGE_COLS + cc + 1) * LANES)

        def natural(i, carry):
            rows = _row_chunk(i, ROW_CHUNK)
            rs = rs_ref[rows, :]
            for cc in range(STAGE_COLS):
                cols = col(cc)
                hn = ((x_ref[rows, cols] * rs * gain_ref[:, cols]) * (1.0 + sc_ref[:, cols])
                      + sh_ref[:, cols])
                h_ref[0, rows, cols] = hn.astype(BF16)
                stage_ref[cc, rows, :] = hn
            return carry

        lax.fori_loop(0, n_chunks, natural, 0)

        for g in range(1, N_GROUPS):
            dil = DILATED_PATTERNS[g][1]
            per = tm // dil

            def permute(r, carry, g=g, dil=dil, per=per):
                dst = _row_chunk(r, per)
                for cc in range(STAGE_COLS):
                    h_ref[g, dst, col(cc)] = (
                        stage_ref[cc, pl.ds(r, per, stride=dil), :].astype(BF16))
                return carry

            lax.fori_loop(0, dil, permute, 0)


def _att_proj_kernel(x_ref, gain_ref, sc_ref, sh_ref, w_ref, hg_ref, o_ref, h_ref, rs_ref,
                     stage_ref, *, d_model):
    j = pl.program_id(1)

    @pl.when(j == 0)
    def _():
        _att_prologue(x_ref, gain_ref, sc_ref, sh_ref, h_ref, rs_ref, stage_ref)

    tn = w_ref.shape[1]
    group = (j * tn) // (3 * d_model)
    acc = jnp.dot(h_ref[group], w_ref[...], preferred_element_type=F32)
    is_v = ((j * tn) // d_model) % 3 == 2

    @pl.when(is_v)
    def _():
        o_ref[...] = acc.astype(BF16)

    @pl.when(jnp.logical_not(is_v))
    def _():
        for hh in range(tn // HEAD_DIM):
            sl = slice(hh * HEAD_DIM, (hh + 1) * HEAD_DIM)
            a = acc[:, sl]
            ms = jnp.mean(a * a, axis=-1, keepdims=True)
            o_ref[:, sl] = (a * lax.rsqrt(ms + NORM_EPS) * hg_ref[:, sl]).astype(BF16)


def _att_proj(x, gain, mod3, mod_row, w, head_gain, tiles_per_batch):
    t, d = x.shape
    n = w.shape[1]
    return pl.pallas_call(
        functools.partial(_att_proj_kernel, d_model=d),
        grid=(t // TM, n // TN),
        in_specs=[pl.BlockSpec((TM, d), lambda i, j: (i, 0)),
                  pl.BlockSpec((1, d), lambda i, j: (0, 0)),
                  _mod_spec(d, tiles_per_batch, mod_row, 1),
                  _mod_spec(d, tiles_per_batch, mod_row, 0),
                  pl.BlockSpec((d, TN), lambda i, j: (0, j)),
                  pl.BlockSpec((1, TN), lambda i, j: (0, j))],
        out_specs=pl.BlockSpec((TM, TN), lambda i, j: (i, j)),
        out_shape=jax.ShapeDtypeStruct((t, n), BF16),
        scratch_shapes=[pltpu.VMEM((N_GROUPS, TM, d), BF16),
                        pltpu.VMEM((TM, LANES), F32),
                        pltpu.VMEM((STAGE_COLS, TM, LANES), F32)],
        compiler_params=_params(2),
        name="att_qkv_proj",
    )(x, gain, mod3, mod3, w, head_gain)


def _proj_kernel(x_ref, gain_ref, sc_ref, sh_ref, w_ref, o_ref, h_ref):
    @pl.when(pl.program_id(1) == 0)
    def _():
        _norm_mod_to_scratch(x_ref, gain_ref, sc_ref, sh_ref, h_ref)

    o_ref[...] = jnp.dot(h_ref[...], w_ref[...], preferred_element_type=F32).astype(BF16)


def _proj(x, gain, mod3, mod_row, w, tiles_per_batch):
    t, d = x.shape
    n = w.shape[1]
    return pl.pallas_call(
        _proj_kernel,
        grid=(t // TM, n // TN),
        in_specs=[pl.BlockSpec((TM, d), lambda i, j: (i, 0)),
                  pl.BlockSpec((1, d), lambda i, j: (0, 0)),
                  _mod_spec(d, tiles_per_batch, mod_row, 1),
                  _mod_spec(d, tiles_per_batch, mod_row, 0),
                  pl.BlockSpec((d, TN), lambda i, j: (0, j))],
        out_specs=pl.BlockSpec((TM, TN), lambda i, j: (i, j)),
        out_shape=jax.ShapeDtypeStruct((t, n), BF16),
        scratch_shapes=[pltpu.VMEM((TM, d), BF16)],
        compiler_params=_params(2),
        name="ret_qkvg_proj",
    )(x, gain, mod3, mod3, w)


def _ffn_up_kernel(x_ref, gain_ref, sc_ref, sh_ref, wg_ref, wu_ref, o_ref, h_ref):
    @pl.when(pl.program_id(1) == 0)
    def _():
        _norm_mod_to_scratch(x_ref, gain_ref, sc_ref, sh_ref, h_ref)

    h = h_ref[...]
    gate = jnp.dot(h, wg_ref[...], preferred_element_type=F32)
    up = jnp.dot(h, wu_ref[...], preferred_element_type=F32)
    o_ref[...] = (_silu(gate) * up).astype(BF16)


def _ffn_up(x, gain, mod3, mod_row, w_up, tiles_per_batch):
    t, d = x.shape
    d_ff = w_up.shape[1] // 2
    tf = 512
    n_f = d_ff // tf
    return pl.pallas_call(
        _ffn_up_kernel,
        grid=(t // TM, n_f),
        in_specs=[pl.BlockSpec((TM, d), lambda i, j: (i, 0)),
                  pl.BlockSpec((1, d), lambda i, j: (0, 0)),
                  _mod_spec(d, tiles_per_batch, mod_row, 4),
                  _mod_spec(d, tiles_per_batch, mod_row, 3),
                  pl.BlockSpec((d, tf), lambda i, j: (0, j)),
                  pl.BlockSpec((d, tf), lambda i, j: (0, j + n_f))],
        out_specs=pl.BlockSpec((TM, tf), lambda i, j: (i, j)),
        out_shape=jax.ShapeDtypeStruct((t, d_ff), BF16),
        scratch_shapes=[pltpu.VMEM((TM, d), BF16)],
        compiler_params=_params(2),
        name="ffn_up",
    )(x, gain, mod3, mod3, w_up, w_up)


def _out_proj_kernel(a_ref, w_ref, x_ref, g_ref, o_ref):
    y = jnp.dot(a_ref[...], w_ref[...], preferred_element_type=F32)
    o_ref[...] = x_ref[...] + g_ref[...] * y


def _gate_spec(n, tn, mod_row, gate_chunk, tiles_per_batch):
    return pl.BlockSpec((None, 1, tn),
                        lambda i, j: (mod_row + i // tiles_per_batch, 0,
                                      gate_chunk * (n // tn) + j))


def _out_proj(a, w, x, mod3, mod_row, gate_chunk, tiles_per_batch, tm, tn, name):
    t, k = a.shape
    n = w.shape[1]
    return pl.pallas_call(
        _out_proj_kernel,
        grid=(t // tm, n // tn),
        in_specs=[pl.BlockSpec((tm, k), lambda i, j: (i, 0)),
                  pl.BlockSpec((k, tn), lambda i, j: (0, j)),
                  pl.BlockSpec((tm, tn), lambda i, j: (i, j)),
                  _gate_spec(n, tn, mod_row, gate_chunk, tiles_per_batch)],
        out_specs=pl.BlockSpec((tm, tn), lambda i, j: (i, j)),
        out_shape=jax.ShapeDtypeStruct((t, n), F32),
        compiler_params=_params(2),
        name=name,
    )(a, w, x, mod3)


def _bias_kernel(tab_ref, bucket_ref, valid_ref, o_ref):
    g = pl.program_id(0)
    h = pl.program_id(1)
    bucket = bucket_ref[...]
    col = g * ATT_HEADS + h
    acc = jnp.zeros(bucket.shape, F32)
    for b in range(REL_BUCKETS):
        acc = jnp.where(bucket == b, tab_ref[b, col], acc)
    for first in range(2):
        o_ref[first] = jnp.where(valid_ref[first] != 0, acc, MASK_VALUE)


def _t5_bucket_np(dist):
    max_exact = REL_BUCKETS // 2
    d_f = np.maximum(dist, 1).astype(np.float32)
    large = max_exact + (np.log(d_f / np.float32(max_exact))
                         / np.float32(math.log(REL_MAX_DIST / max_exact))
                         * np.float32(REL_BUCKETS - max_exact)).astype(np.int32)
    large = np.minimum(large, REL_BUCKETS - 1)
    return np.where(dist < max_exact, dist, large).astype(np.int32)


def _bias_tiles(rel_bias):
    blk = ATT_BLK
    qi = np.arange(blk)[:, None]
    ki = np.arange(2 * blk)[None, :]
    delta = blk + qi - ki
    band = (delta >= 0) & (delta <= blk)
    valid = np.stack([band & (ki >= blk), band]).astype(np.int32)
    buckets = np.stack([_t5_bucket_np(np.maximum(delta, 0) * dil)
                        for _, dil in DILATED_PATTERNS])
    return pl.pallas_call(
        _bias_kernel,
        grid=(N_GROUPS, ATT_HEADS),
        in_specs=[pl.BlockSpec(memory_space=pltpu.SMEM),
                  pl.BlockSpec((None, blk, 2 * blk), lambda g, h: (g, 0, 0)),
                  pl.BlockSpec((2, blk, 2 * blk), lambda g, h: (0, 0, 0))],
        out_specs=pl.BlockSpec((None, 2, None, blk, 2 * blk), lambda g, h: (g, 0, h, 0, 0)),
        out_shape=jax.ShapeDtypeStruct((N_GROUPS, 2, ATT_HEADS, blk, 2 * blk), F32),
        compiler_params=_params(2),
        name="rel_bias_tiles",
    )(rel_bias, jnp.asarray(buckets), jnp.asarray(valid))


ATT_HEAD_BATCH = 4


def _att_units(q_ref, kc_ref, kp_ref, vc_ref, vp_ref, bias_ref, store, *, g, heads, first_span):
    blk = ATT_BLK
    dil = DILATED_PATTERNS[g][1]
    dn = (((1,), (1,)), ((), ()))
    ones = jnp.ones((2 * blk, HEAD_DIM), BF16)
    lane = lax.broadcasted_iota(jnp.int32, (blk, LANES), 1)
    head0 = pl.program_id(2) * heads

    def unit(u, q_of, k_of, v_of, variant):
        lse_tile = jnp.zeros((blk, LANES), F32)
        for hb in range(heads // ATT_HEAD_BATCH):
            hs = range(hb * ATT_HEAD_BATCH, (hb + 1) * ATT_HEAD_BATCH)
            sl = lambda h: slice(h * HEAD_DIM, (h + 1) * HEAD_DIM)
            s = jnp.concatenate(
                [lax.dot_general(q_of(sl(h)), k_of(sl(h)), dn, preferred_element_type=F32)
                 for h in hs], axis=0)
            s = s + bias_ref[variant, hs.start:hs.stop].reshape(ATT_HEAD_BATCH * blk, 2 * blk)
            m = jnp.max(jnp.maximum(s[:, :blk], s[:, blk:]), axis=-1, keepdims=True)
            p = jnp.exp(s - m).astype(BF16)
            for idx, h in enumerate(hs):
                rows = slice(idx * blk, (idx + 1) * blk)
                v_ext = jnp.concatenate([v_of(sl(h)), ones], axis=1)
                pv = jnp.dot(p[rows], v_ext, preferred_element_type=F32)
                denom = pv[:, HEAD_DIM:]
                o = pv[:, :HEAD_DIM] / denom
                lse_tile = jnp.where(lane == head0 + h, m[rows] + jnp.log(denom), lse_tile)
                store(u, h, o, None)
        store(u, None, None, lse_tile)

    if dil == 1:
        n_units = q_ref.shape[0] // blk
        unit(0,
             lambda sl: q_ref[0:blk, sl],
             lambda sl: jnp.concatenate([kp_ref[:, sl], kc_ref[0:blk, sl]], axis=0),
             lambda sl: jnp.concatenate([vp_ref[:, sl], vc_ref[0:blk, sl]], axis=0),
             jnp.where(first_span, 0, 1))

        def body(u, carry):
            cur = _row_chunk(u, blk)
            both = pl.ds(pl.multiple_of((u - 1) * blk, blk), 2 * blk)
            unit(u, lambda sl: q_ref[cur, sl], lambda sl: kc_ref[both, sl],
                 lambda sl: vc_ref[both, sl], 1)
            return carry

        lax.fori_loop(1, n_units, body, 0)
    else:
        variant = jnp.where(first_span, 0, 1)
        if dil == 4:
            pick = lambda ref, r, sl: ref[r, :, sl]
            prev_cur = lambda pref, cref, r, sl: jnp.concatenate(
                [pref[r, :, sl], cref[r, :, sl]], axis=0)
        else:
            pick = lambda ref, r, sl: jnp.concatenate([ref[0, r, :, sl], ref[1, r, :, sl]], axis=0)
            prev_cur = lambda pref, cref, r, sl: jnp.concatenate(
                [pref[0, r, :, sl], pref[1, r, :, sl], cref[0, r, :, sl], cref[1, r, :, sl]],
                axis=0)

        def body(r, carry):
            unit(r, lambda sl: pick(q_ref, r, sl), lambda sl: prev_cur(kp_ref, kc_ref, r, sl),
                 lambda sl: prev_cur(vp_ref, vc_ref, r, sl), variant)
            return carry

        lax.fori_loop(0, dil, body, 0)


def _att_kernel(q_ref, kc_ref, kp_ref, vc_ref, vp_ref, bias_ref, o_ref, lse_ref, *scratch,
                g, heads):
    blk = ATT_BLK
    dil = DILATED_PATTERNS[g][1]
    first_span = pl.program_id(1) == 0

    if dil == 1:
        def store(u, h, o, lse_tile):
            rows = _row_chunk(u, blk)
            if h is None:
                lse_ref[rows, :] = lse_tile
            else:
                o_ref[rows, h * HEAD_DIM:(h + 1) * HEAD_DIM] = o.astype(BF16)
    else:
        o_scr, lse_scr = scratch

        def store(u, h, o, lse_tile):
            rows = pl.ds(u, blk, stride=dil)
            if h is None:
                lse_scr[rows, :] = lse_tile
            else:
                o_scr[h, rows, :] = o

    _att_units(q_ref, kc_ref, kp_ref, vc_ref, vp_ref, bias_ref, store, g=g, heads=heads,
               first_span=first_span)

    if dil != 1:
        for h in range(heads):
            o_ref[:, h * HEAD_DIM:(h + 1) * HEAD_DIM] = o_scr[h].astype(BF16)
        lse_ref[...] = lse_scr[...]


def _attention_group(qkv, bias, g, batch, seq):
    dil = DILATED_PATTERNS[g][1]
    blk = ATT_BLK
    span = dil * blk if dil > 1 else 4 * blk
    heads = ATT_HEADS if dil < 16 else 4
    n_hg = ATT_HEADS // heads
    width = heads * HEAD_DIM
    n_cols = qkv.shape[1]
    col0 = lambda which: (g * 3 + which) * (ATT_HEADS // heads)

    if dil == 1:
        view = qkv.reshape(batch, seq, n_cols)
        cur = lambda which: pl.BlockSpec((None, span, width),
                                         lambda b, s, h: (b, s, col0(which) + h))
        prev = lambda which: pl.BlockSpec(
            (None, blk, width),
            lambda b, s, h: (b, jnp.maximum(s * (span // blk) - 1, 0), col0(which) + h))
        scratch = []
    elif dil == 4:
        halves = TM // (dil * blk)
        view = qkv.reshape(batch, seq // TM, dil, halves, blk, n_cols)
        spec = lambda which, shift: pl.BlockSpec(
            (None, None, dil, None, blk, width),
            lambda b, s, h: (b, jnp.maximum(s - shift, 0) // halves, 0,
                             jnp.maximum(s - shift, 0) % halves, 0, col0(which) + h))
        cur = lambda which: spec(which, 0)
        prev = lambda which: spec(which, 1)
        scratch = [pltpu.VMEM((heads, span, HEAD_DIM), F32), pltpu.VMEM((span, LANES), F32)]
    else:
        tiles = span // TM
        per = TM // dil
        view = qkv.reshape(batch, seq // span, tiles, dil, per, n_cols)
        spec = lambda which, shift: pl.BlockSpec(
            (None, None, tiles, dil, per, width),
            lambda b, s, h: (b, jnp.maximum(s - shift, 0), 0, 0, 0, col0(which) + h))
        cur = lambda which: spec(which, 0)
        prev = lambda which: spec(which, 1)
        scratch = [pltpu.VMEM((heads, span, HEAD_DIM), F32), pltpu.VMEM((span, LANES), F32)]

    o, lse = pl.pallas_call(
        functools.partial(_att_kernel, g=g, heads=heads),
        grid=(batch, seq // span, n_hg),
        in_specs=[cur(0), cur(1), prev(1), cur(2), prev(2),
                  pl.BlockSpec((None, 2, heads, blk, 2 * blk), lambda b, s, h: (g, 0, h, 0, 0))],
        out_specs=[pl.BlockSpec((None, span, width), lambda b, s, h: (b, s, h)),
                   pl.BlockSpec((None, None, span, LANES), lambda b, s, h: (b, h, s, 0))],
        out_shape=[jax.ShapeDtypeStruct((batch, seq, ATT_HEADS * HEAD_DIM), BF16),
                   jax.ShapeDtypeStruct((batch, n_hg, seq, LANES), F32)],
        scratch_shapes=scratch,
        compiler_params=_params(3),
        name=f"dilated_attention_g{g}",
    )(view, view, view, view, view, bias)
    return o, lse


MERGE_ROWS = 256


def _att_out_kernel(o0_ref, o1_ref, o2_ref, l0_ref, l1_ref, l2_ref, w_ref, x_ref, g_ref,
                    out_ref):
    o_refs = (o0_ref, o1_ref, o2_ref)
    l_refs = (l0_ref, l1_ref, l2_ref)
    tm = x_ref.shape[0]
    lane = lax.broadcasted_iota(jnp.int32, (MERGE_ROWS, LANES), 1)
    for rc in range(tm // MERGE_ROWS):
        rows = slice(rc * MERGE_ROWS, (rc + 1) * MERGE_ROWS)
        lses = []
        for l_ref in l_refs:
            tile = l_ref[0, rows, :]
            for hg in range(1, l_ref.shape[0]):
                tile = tile + l_ref[hg, rows, :]
            lses.append(tile)
        mx = jnp.maximum(jnp.maximum(lses[0], lses[1]), lses[2])
        es = [jnp.exp(t - mx) for t in lses]
        den = es[0] + es[1] + es[2]
        ws = [e / den for e in es]
        pieces = []
        for h in range(ATT_HEADS):
            sl = slice(h * HEAD_DIM, (h + 1) * HEAD_DIM)
            acc = None
            for w_g, o_ref in zip(ws, o_refs):
                w_col = jnp.sum(jnp.where(lane == h, w_g, 0.0), axis=-1, keepdims=True)
                term = w_col * o_ref[rows, sl].astype(F32)
                acc = term if acc is None else acc + term
            pieces.append(acc.astype(BF16))
        merged = jnp.concatenate(pieces, axis=1)
        y = jnp.dot(merged, w_ref[...], preferred_element_type=F32)
        out_ref[rows, :] = x_ref[rows, :] + g_ref[...] * y


def _att_out_proj(outs, lses, w, x, mod3, mod_row, tiles_per_batch_tm, tm):
    t, n = x.shape
    k = w.shape[0]
    seq_tiles = tiles_per_batch_tm
    o_spec = pl.BlockSpec((tm, k), lambda i, j: (i, 0))
    l_spec = lambda n_hg: pl.BlockSpec((None, n_hg, tm, LANES),
                                       lambda i, j: (i // seq_tiles, 0, i % seq_tiles, 0))
    return pl.pallas_call(
        _att_out_kernel,
        grid=(t // tm, 1),
        in_specs=[o_spec, o_spec, o_spec] + [l_spec(l.shape[1]) for l in lses]
                 + [pl.BlockSpec((k, n), lambda i, j: (0, 0)),
                    pl.BlockSpec((tm, n), lambda i, j: (i, 0)),
                    _gate_spec(n, n, mod_row, 2, seq_tiles)],
        out_specs=pl.BlockSpec((tm, n), lambda i, j: (i, 0)),
        out_shape=jax.ShapeDtypeStruct((t, n), F32),
        compiler_params=_params(2),
        name="att_merge_out_proj",
    )(*[o.reshape(t, k) for o in outs], *lses, w, x, mod3)


RET_TOKENS = 512


def _retention_kernel(q_ref, k_ref, v_ref, g_ref, cos_ref, sin_ref, inner_ref, cross_ref,
                      sdec_ref, cdec_ref, o_ref, state_ref):
    @pl.when(pl.program_id(2) == 0)
    def _():
        state_ref[...] = jnp.zeros_like(state_ref)

    half = RET_HEAD_QK // 2
    dn_nt = (((1,), (1,)), ((), ()))
    dn_tn = (((0,), (0,)), ((), ()))
    inner = inner_ref[...]
    cross = cross_ref[...]
    sdec = sdec_ref[...]
    cdec = cdec_ref[...]
    k_scale = RET_HEAD_QK ** -0.5

    for c in range(RET_TOKENS // RET_CHUNK):
        rows = slice(c * RET_CHUNK, (c + 1) * RET_CHUNK)
        cos = cos_ref[rows, :]
        sin = sin_ref[rows, :]
        q = q_ref[rows, :].astype(F32)
        k = k_ref[rows, :].astype(F32)
        q1, q2 = q[:, :half], q[:, half:]
        k1, k2 = k[:, :half], k[:, half:]
        qr1 = q1 * cos - q2 * sin
        qr2 = q1 * sin + q2 * cos
        kr1 = (k1 * cos - k2 * sin) * k_scale
        kr2 = (k1 * sin + k2 * cos) * k_scale
        q_b = jnp.concatenate([qr1, qr2], axis=1).astype(BF16)
        qc_b = jnp.concatenate([qr1 * cross, qr2 * cross], axis=1).astype(BF16)
        k_b = jnp.concatenate([kr1, kr2], axis=1).astype(BF16)
        ks_b = jnp.concatenate([kr1 * sdec, kr2 * sdec], axis=1).astype(BF16)
        v = v_ref[rows, :]

        scores = lax.dot_general(q_b, k_b, dn_nt, preferred_element_type=F32) * inner
        state = state_ref[...]
        o = (jnp.dot(scores.astype(BF16), v, preferred_element_type=F32)
             + jnp.dot(qc_b, state.astype(BF16), preferred_element_type=F32))
        state_ref[...] = state * cdec + lax.dot_general(ks_b, v, dn_tn,
                                                        preferred_element_type=F32)

        ms = jnp.mean(o * o, axis=-1, keepdims=True)
        gate = g_ref[rows, :].astype(F32)
        o_ref[rows, :] = (_silu(gate) * (o * lax.rsqrt(ms + NORM_EPS))).astype(BF16)


def _retention(qkvg, batch, seq):
    c = RET_CHUNK
    hq, hv = RET_HEAD_QK, RET_HEAD_V
    n_qk = RET_HEADS * hq
    n_v = RET_HEADS * hv

    half = hq // 2
    inv = 1.0 / (10000.0 ** jnp.linspace(0.0, 1.0, half, dtype=F32))
    ang = jnp.arange(seq, dtype=F32)[:, None] * inv[None, :]
    cos, sin = jnp.cos(ang), jnp.sin(ang)
    log_gamma = jnp.log(1.0 - 2.0 ** (-5.0 - jnp.arange(RET_HEADS, dtype=F32)))
    pos = jnp.arange(c, dtype=F32)
    diff = pos[:, None] - pos[None, :]
    inner = jnp.where(diff[None] >= 0,
                      jnp.exp(jnp.maximum(diff, 0.0)[None] * log_gamma[:, None, None]), 0.0)
    cross = jnp.exp((pos[None, :] + 1.0) * log_gamma[:, None])
    sdec = jnp.exp((c - 1.0 - pos)[None, :] * log_gamma[:, None])
    cdec = jnp.exp(c * log_gamma)
    cross_l = jnp.broadcast_to(cross[:, :, None], (RET_HEADS, c, half))
    sdec_l = jnp.broadcast_to(sdec[:, :, None], (RET_HEADS, c, half))
    cdec_l = jnp.broadcast_to(cdec[:, None, None], (RET_HEADS, 1, hv))

    x = qkvg.reshape(batch, seq, qkvg.shape[1])
    tt = RET_TOKENS
    head_tab = lambda shape: pl.BlockSpec((None,) + shape, lambda b, h, s: (h, 0, 0))
    out = pl.pallas_call(
        _retention_kernel,
        grid=(batch, RET_HEADS, seq // tt),
        in_specs=[pl.BlockSpec((None, tt, hq), lambda b, h, s: (b, s, h)),
                  pl.BlockSpec((None, tt, hq), lambda b, h, s: (b, s, n_qk // hq + h)),
                  pl.BlockSpec((None, tt, hv), lambda b, h, s: (b, s, 2 * n_qk // hv + h)),
                  pl.BlockSpec((None, tt, hv), lambda b, h, s: (b, s, (2 * n_qk + n_v) // hv + h)),
                  pl.BlockSpec((tt, half), lambda b, h, s: (s, 0)),
                  pl.BlockSpec((tt, half), lambda b, h, s: (s, 0)),
                  head_tab((c, c)), head_tab((c, half)), head_tab((c, half)), head_tab((1, hv))],
        out_specs=pl.BlockSpec((None, tt, hv), lambda b, h, s: (b, s, h)),
        out_shape=jax.ShapeDtypeStruct((batch, seq, n_v), BF16),
        scratch_shapes=[pltpu.VMEM((hq, hv), F32)],
        compiler_params=_params(3),
        name="retention",
    )(x, x, x, x, cos, sin, inner, cross_l, sdec_l, cdec_l)
    return out.reshape(batch * seq, n_v)


def kernel(x, c, w_mod, b_mod, norm_mix, norm_ffn, rel_bias, att_w_qkv, att_q_gain, att_k_gain,
           att_w_o, ret_w_qkvg, ret_w_o, ffn_w_up, ffn_w_down):
    batch, seq, d = x.shape
    depth = w_mod.shape[0]
    assert seq % (2 * TM) == 0 and d == ATT_HEADS * HEAD_DIM
    tiles_per_batch = seq // TM

    mod = _modulation(c, w_mod, b_mod)
    mod3 = mod.reshape(depth * batch, 1, 6 * d)
    bias = _bias_tiles(rel_bias)
    xf = x.reshape(batch * seq, d)

    for i in range(depth):
        jdx = i // 2
        mod_row = i * batch
        gain_mix = norm_mix[i].reshape(1, d)
        if i % 2 == 0:
            scale = HEAD_DIM ** -0.5
            ones = jnp.ones((ATT_HEADS * HEAD_DIM,), F32)
            head_gain = jnp.concatenate(
                [jnp.concatenate([jnp.tile(att_q_gain[jdx, g] * scale, ATT_HEADS),
                                  jnp.tile(att_k_gain[jdx, g], ATT_HEADS), ones])
                 for g in range(N_GROUPS)]).reshape(1, -1)
            qkv = _att_proj(xf, gain_mix, mod3, mod_row, att_w_qkv[jdx].astype(BF16), head_gain,
                            tiles_per_batch)
            outs, lses = zip(*[_attention_group(qkv, bias, g, batch, seq)
                               for g in range(N_GROUPS)])
            att_tm = 512
            xf = _att_out_proj(outs, lses, att_w_o[jdx].astype(BF16), xf, mod3, mod_row,
                               seq // att_tm, att_tm)
        else:
            qkvg = _proj(xf, gain_mix, mod3, mod_row, ret_w_qkvg[jdx].astype(BF16),
                         tiles_per_batch)
            mixed = _retention(qkvg, batch, seq)
            xf = _out_proj(mixed, ret_w_o[jdx].astype(BF16), xf, mod3, mod_row, 2,
                           tiles_per_batch, TM, TN, "ret_out_proj")

        act = _ffn_up(xf, norm_ffn[i].reshape(1, d), mod3, mod_row, ffn_w_up[i].astype(BF16),
                      tiles_per_batch)
        xf = _out_proj(act, ffn_w_down[i].astype(BF16), xf, mod3, mod_row, 5, tiles_per_batch,
                       TM, 512, "ffn_down")
    return xf.reshape(batch, seq, d)
```

```python
import functools
import math

import numpy as np
import jax
import jax.numpy as jnp
from jax import lax
from jax.experimental import pallas as pl
from jax.experimental.pallas import tpu as pltpu

F32 = jnp.float32
BF16 = jnp.bfloat16

NORM_EPS = 1e-6
MASK_VALUE = -1e30
LOG2E = math.log2(math.e)
LANES = 128

ATT_HEADS = 16
HEAD_DIM = 128
DILATED_PATTERNS = ((128, 1), (512, 4), (2048, 16))
N_GROUPS = len(DILATED_PATTERNS)
ATT_BLK = 128
REL_BUCKETS = 32
REL_MAX_DIST = 2048

RET_HEADS = 8
RET_HEAD_QK = 256
RET_HEAD_V = 512
RET_CHUNK = 128

VMEM_LIMIT_BYTES = 56 * 1024 * 1024

TM = 1024
TN = 1024
ROW_CHUNK = 256
MM_ROWS = 256
STAGE_COLS = 4


def _params(n_axes):
    return pltpu.CompilerParams(dimension_semantics=("arbitrary",) * n_axes,
                                vmem_limit_bytes=VMEM_LIMIT_BYTES)


def _silu(v):
    return v / (1.0 + jnp.exp(-v))


def _row_chunk(i, size):
    if isinstance(i, int):
        return pl.ds(i * size, size)
    return pl.ds(pl.multiple_of(i * size, size), size)


def _mod_kernel(c_ref, w_ref, b_ref, o_ref):
    cond = _silu(c_ref[...]).astype(BF16)
    o_ref[...] = (jnp.dot(cond, w_ref[...].astype(BF16), preferred_element_type=F32)
                  + b_ref[...])


def _modulation(c, w_mod, b_mod):
    depth, d, n = w_mod.shape
    b = c.shape[0]
    tn = 1024
    return pl.pallas_call(
        _mod_kernel,
        grid=(depth, n // tn),
        in_specs=[pl.BlockSpec((b, d), lambda l, j: (0, 0)),
                  pl.BlockSpec((None, d, tn), lambda l, j: (l, 0, j)),
                  pl.BlockSpec((None, 1, tn), lambda l, j: (l, 0, j))],
        out_specs=pl.BlockSpec((None, b, tn), lambda l, j: (l, 0, j)),
        out_shape=jax.ShapeDtypeStruct((depth, b, n), F32),
        compiler_params=_params(2),
        name="adaln_mod",
    )(c, w_mod, b_mod.reshape(depth, 1, n))


def _row_rsqrt_to_scratch(x_ref, rs_ref):
    def body(i, carry):
        rows = _row_chunk(i, ROW_CHUNK)
        xx = x_ref[rows, :]
        ms = jnp.mean(xx * xx, axis=-1, keepdims=True)
        rs_ref[rows, :] = jnp.broadcast_to(lax.rsqrt(ms + NORM_EPS), (ROW_CHUNK, LANES))
        return carry

    lax.fori_loop(0, x_ref.shape[0] // ROW_CHUNK, body, 0)


def _norm_mod_to_scratch(x_ref, gain_ref, sc_ref, sh_ref, h_ref, rs_ref):
    _row_rsqrt_to_scratch(x_ref, rs_ref)

    def body(i, carry):
        rows = _row_chunk(i, ROW_CHUNK)
        rs = rs_ref[rows, :]
        for cb in range(x_ref.shape[1] // LANES):
            cols = slice(cb * LANES, (cb + 1) * LANES)
            hn = ((x_ref[rows, cols] * rs * gain_ref[:, cols]) * (1.0 + sc_ref[:, cols])
                  + sh_ref[:, cols])
            h_ref[rows, cols] = hn.astype(BF16)
        return carry

    lax.fori_loop(0, x_ref.shape[0] // ROW_CHUNK, body, 0)


def _mod_spec(d, tiles_per_batch, row, chunk):
    return pl.BlockSpec((None, 1, d), lambda i, j: (row + i // tiles_per_batch, 0, chunk))


def _att_prologue(x_ref, gain_ref, sc_ref, sh_ref, h_ref, rs_ref, stage_ref):
    tm, d = x_ref.shape
    n_chunks = tm // ROW_CHUNK

    _row_rsqrt_to_scratch(x_ref, rs_ref)

    for cg in range(d // (STAGE_COLS * LANES)):
        col = lambda cc: slice((cg * STAGE_COLS + cc) * LANES, (cg * STAGE_COLS + cc + 1) * LANES)

        def natural(i, carry):
            rows = _row_chunk(i, ROW_CHUNK)
            rs = rs_ref[rows, :]
            for cc in range(STAGE_COLS):
                cols = col(cc)
                hn = ((x_ref[rows, cols] * rs * gain_ref[:, cols]) * (1.0 + sc_ref[:, cols])
                      + sh_ref[:, cols])
                h_ref[0, rows, cols] = hn.astype(BF16)
                stage_ref[cc, rows, :] = hn
            return carry

        lax.fori_loop(0, n_chunks, natural, 0)

        for g in range(1, N_GROUPS):
            dil = DILATED_PATTERNS[g][1]
            per = tm // dil

            def permute(r, carry, g=g, dil=dil, per=per):
                dst = _row_chunk(r, per)
                for cc in range(STAGE_COLS):
                    h_ref[g, dst, col(cc)] = (
                        stage_ref[cc, pl.ds(r, per, stride=dil), :].astype(BF16))
                return carry

            lax.fori_loop(0, dil, permute, 0)


def _att_proj_kernel(x_ref, gain_ref, sc_ref, sh_ref, w_ref, hg_ref, o_ref, h_ref, rs_ref,
                     stage_ref, *, d_model):
    j = pl.program_id(1)

    @pl.when(j == 0)
    def _():
        _att_prologue(x_ref, gain_ref, sc_ref, sh_ref, h_ref, rs_ref, stage_ref)

    tn = w_ref.shape[1]
    group = (j * tn) // (3 * d_model)
    is_v = ((j * tn) // d_model) % 3 == 2

    def tile(head_norm):
        for rc in range(h_ref.shape[1] // MM_ROWS):
            rows = slice(rc * MM_ROWS, (rc + 1) * MM_ROWS)
            acc = jnp.dot(h_ref[group, rows, :], w_ref[...], preferred_element_type=F32)
            if not head_norm:
                o_ref[rows, :] = acc.astype(BF16)
                continue
            for hh in range(tn // HEAD_DIM):
                sl = slice(hh * HEAD_DIM, (hh + 1) * HEAD_DIM)
                a = acc[:, sl]
                ms = jnp.mean(a * a, axis=-1, keepdims=True)
                o_ref[rows, sl] = (a * lax.rsqrt(ms + NORM_EPS) * hg_ref[:, sl]).astype(BF16)

    pl.when(is_v)(functools.partial(tile, False))
    pl.when(jnp.logical_not(is_v))(functools.partial(tile, True))


def _att_proj(x, gain, mod3, mod_row, w, head_gain, tiles_per_batch):
    t, d = x.shape
    n = w.shape[1]
    return pl.pallas_call(
        functools.partial(_att_proj_kernel, d_model=d),
        grid=(t // TM, n // TN),
        in_specs=[pl.BlockSpec((TM, d), lambda i, j: (i, 0)),
                  pl.BlockSpec((1, d), lambda i, j: (0, 0)),
                  _mod_spec(d, tiles_per_batch, mod_row, 1),
                  _mod_spec(d, tiles_per_batch, mod_row, 0),
                  pl.BlockSpec((d, TN), lambda i, j: (0, j)),
                  pl.BlockSpec((1, TN), lambda i, j: (0, j))],
        out_specs=pl.BlockSpec((TM, TN), lambda i, j: (i, j)),
        out_shape=jax.ShapeDtypeStruct((t, n), BF16),
        scratch_shapes=[pltpu.VMEM((N_GROUPS, TM, d), BF16),
                        pltpu.VMEM((TM, LANES), F32),
                        pltpu.VMEM((STAGE_COLS, TM, LANES), F32)],
        compiler_params=_params(2),
        name="att_qkv_proj",
    )(x, gain, mod3, mod3, w, head_gain)


def _proj_kernel(x_ref, gain_ref, sc_ref, sh_ref, w_ref, o_ref, h_ref, rs_ref, *, gate_col0):
    j = pl.program_id(1)

    @pl.when(j == 0)
    def _():
        _norm_mod_to_scratch(x_ref, gain_ref, sc_ref, sh_ref, h_ref, rs_ref)

    is_gate = j * w_ref.shape[1] >= gate_col0

    def tile(gate):
        for rc in range(h_ref.shape[0] // MM_ROWS):
            rows = slice(rc * MM_ROWS, (rc + 1) * MM_ROWS)
            acc = jnp.dot(h_ref[rows, :], w_ref[...], preferred_element_type=F32)
            o_ref[rows, :] = (_silu(acc) if gate else acc).astype(BF16)

    pl.when(is_gate)(functools.partial(tile, True))
    pl.when(jnp.logical_not(is_gate))(functools.partial(tile, False))


def _proj(x, gain, mod3, mod_row, w, tiles_per_batch, gate_col0):
    t, d = x.shape
    n = w.shape[1]
    return pl.pallas_call(
        functools.partial(_proj_kernel, gate_col0=gate_col0),
        grid=(t // TM, n // TN),
        in_specs=[pl.BlockSpec((TM, d), lambda i, j: (i, 0)),
                  pl.BlockSpec((1, d), lambda i, j: (0, 0)),
                  _mod_spec(d, tiles_per_batch, mod_row, 1),
                  _mod_spec(d, tiles_per_batch, mod_row, 0),
                  pl.BlockSpec((d, TN), lambda i, j: (0, j))],
        out_specs=pl.BlockSpec((TM, TN), lambda i, j: (i, j)),
        out_shape=jax.ShapeDtypeStruct((t, n), BF16),
        scratch_shapes=[pltpu.VMEM((TM, d), BF16), pltpu.VMEM((TM, LANES), F32)],
        compiler_params=_params(2),
        name="ret_qkvg_proj",
    )(x, gain, mod3, mod3, w)


def _ffn_up_kernel(x_ref, gain_ref, sc_ref, sh_ref, wg_ref, wu_ref, o_ref, h_ref, rs_ref):
    @pl.when(pl.program_id(1) == 0)
    def _():
        _norm_mod_to_scratch(x_ref, gain_ref, sc_ref, sh_ref, h_ref, rs_ref)

    for rc in range(h_ref.shape[0] // MM_ROWS):
        rows = slice(rc * MM_ROWS, (rc + 1) * MM_ROWS)
        h = h_ref[rows, :]
        gate = jnp.dot(h, wg_ref[...], preferred_element_type=F32)
        up = jnp.dot(h, wu_ref[...], preferred_element_type=F32)
        o_ref[rows, :] = (_silu(gate) * up).astype(BF16)


def _ffn_up(x, gain, mod3, mod_row, w_up, tiles_per_batch):
    t, d = x.shape
    d_ff = w_up.shape[1] // 2
    tf = 512
    n_f = d_ff // tf
    return pl.pallas_call(
        _ffn_up_kernel,
        grid=(t // TM, n_f),
        in_specs=[pl.BlockSpec((TM, d), lambda i, j: (i, 0)),
                  pl.BlockSpec((1, d), lambda i, j: (0, 0)),
                  _mod_spec(d, tiles_per_batch, mod_row, 4),
                  _mod_spec(d, tiles_per_batch, mod_row, 3),
                  pl.BlockSpec((d, tf), lambda i, j: (0, j)),
                  pl.BlockSpec((d, tf), lambda i, j: (0, j + n_f))],
        out_specs=pl.BlockSpec((TM, tf), lambda i, j: (i, j)),
        out_shape=jax.ShapeDtypeStruct((t, d_ff), BF16),
        scratch_shapes=[pltpu.VMEM((TM, d), BF16), pltpu.VMEM((TM, LANES), F32)],
        compiler_params=_params(2),
        name="ffn_up",
    )(x, gain, mod3, mod3, w_up, w_up)


def _out_proj_kernel(a_ref, w_ref, x_ref, g_ref, o_ref):
    for rc in range(a_ref.shape[0] // MM_ROWS):
        rows = slice(rc * MM_ROWS, (rc + 1) * MM_ROWS)
        y = jnp.dot(a_ref[rows, :], w_ref[...], preferred_element_type=F32)
        o_ref[rows, :] = x_ref[rows, :] + g_ref[...] * y


def _gate_spec(n, tn, mod_row, gate_chunk, tiles_per_batch):
    return pl.BlockSpec((None, 1, tn),
                        lambda i, j: (mod_row + i // tiles_per_batch, 0,
                                      gate_chunk * (n // tn) + j))


def _out_proj(a, w, x, mod3, mod_row, gate_chunk, tiles_per_batch, tm, tn, name):
    t, k = a.shape
    n = w.shape[1]
    return pl.pallas_call(
        _out_proj_kernel,
        grid=(t // tm, n // tn),
        in_specs=[pl.BlockSpec((tm, k), lambda i, j: (i, 0)),
                  pl.BlockSpec((k, tn), lambda i, j: (0, j)),
                  pl.BlockSpec((tm, tn), lambda i, j: (i, j)),
                  _gate_spec(n, tn, mod_row, gate_chunk, tiles_per_batch)],
        out_specs=pl.BlockSpec((tm, tn), lambda i, j: (i, j)),
        out_shape=jax.ShapeDtypeStruct((t, n), F32),
        compiler_params=_params(2),
        name=name,
    )(a, w, x, mod3)


def _bias_kernel(tab_ref, bucket_ref, valid_ref, o_ref):
    g = pl.program_id(0)
    h = pl.program_id(1)
    bucket = bucket_ref[...]
    col = g * ATT_HEADS + h
    acc = jnp.zeros(bucket.shape, F32)
    for b in range(REL_BUCKETS):
        acc = jnp.where(bucket == b, tab_ref[b, col] * LOG2E, acc)
    for first in range(2):
        o_ref[first] = jnp.where(valid_ref[first] != 0, acc, MASK_VALUE)


def _t5_bucket_np(dist):
    max_exact = REL_BUCKETS // 2
    d_f = np.maximum(dist, 1).astype(np.float32)
    large = max_exact + (np.log(d_f / np.float32(max_exact))
                         / np.float32(math.log(REL_MAX_DIST / max_exact))
                         * np.float32(REL_BUCKETS - max_exact)).astype(np.int32)
    large = np.minimum(large, REL_BUCKETS - 1)
    return np.where(dist < max_exact, dist, large).astype(np.int32)


def _bias_tiles(rel_bias):
    blk = ATT_BLK
    qi = np.arange(blk)[:, None]
    ki = np.arange(2 * blk)[None, :]
    delta = blk + qi - ki
    band = (delta >= 0) & (delta <= blk)
    valid = np.stack([band & (ki >= blk), band]).astype(np.int32)
    buckets = np.stack([_t5_bucket_np(np.maximum(delta, 0) * dil)
                        for _, dil in DILATED_PATTERNS])
    return pl.pallas_call(
        _bias_kernel,
        grid=(N_GROUPS, ATT_HEADS),
        in_specs=[pl.BlockSpec(memory_space=pltpu.SMEM),
                  pl.BlockSpec((None, blk, 2 * blk), lambda g, h: (g, 0, 0)),
                  pl.BlockSpec((2, blk, 2 * blk), lambda g, h: (0, 0, 0))],
        out_specs=pl.BlockSpec((None, 2, None, blk, 2 * blk), lambda g, h: (g, 0, h, 0, 0)),
        out_shape=jax.ShapeDtypeStruct((N_GROUPS, 2, ATT_HEADS, blk, 2 * blk), F32),
        compiler_params=_params(2),
        name="rel_bias_tiles",
    )(rel_bias, jnp.asarray(buckets), jnp.asarray(valid))


ATT_HEAD_BATCH = 4


def _att_units(q_ref, kc_ref, kp_ref, vc_ref, vp_ref, bias_ref, store, *, g, heads, first_span):
    blk = ATT_BLK
    dil = DILATED_PATTERNS[g][1]
    dn = (((1,), (1,)), ((), ()))
    ones = jnp.ones((2 * blk, HEAD_DIM), BF16)
    lane = lax.broadcasted_iota(jnp.int32, (blk, LANES), 1)
    head0 = pl.program_id(1) * heads

    def unit(u, q_of, k_of, v_of, variant):
        m_tile = jnp.zeros((blk, LANES), F32)
        l_tile = jnp.ones((blk, LANES), F32)
        for hb in range(heads // ATT_HEAD_BATCH):
            hs = range(hb * ATT_HEAD_BATCH, (hb + 1) * ATT_HEAD_BATCH)
            sl = lambda h: slice(h * HEAD_DIM, (h + 1) * HEAD_DIM)
            s = jnp.concatenate(
                [lax.dot_general(q_of(sl(h)), k_of(sl(h)), dn, preferred_element_type=F32)
                 for h in hs], axis=0)
            s = s + bias_ref[variant, hs.start:hs.stop].reshape(ATT_HEAD_BATCH * blk, 2 * blk)
            m = jnp.max(jnp.maximum(s[:, :blk], s[:, blk:]), axis=-1, keepdims=True)
            p = jnp.exp2(s - m).astype(BF16)
            for idx, h in enumerate(hs):
                rows = slice(idx * blk, (idx + 1) * blk)
                v_ext = jnp.concatenate([v_of(sl(h)), ones], axis=1)
                pv = jnp.dot(p[rows], v_ext, preferred_element_type=F32)
                denom = pv[:, HEAD_DIM:]
                o = pv[:, :HEAD_DIM] / denom
                mine = lane == head0 + h
                m_tile = jnp.where(mine, m[rows], m_tile)
                l_tile = jnp.where(mine, denom, l_tile)
                store(u, h, o, None)
        store(u, None, None, m_tile + jnp.log2(l_tile))

    if dil == 1:
        n_units = q_ref.shape[0] // blk
        unit(0,
             lambda sl: q_ref[0:blk, sl],
             lambda sl: jnp.concatenate([kp_ref[:, sl], kc_ref[0:blk, sl]], axis=0),
             lambda sl: jnp.concatenate([vp_ref[:, sl], vc_ref[0:blk, sl]], axis=0),
             jnp.where(first_span, 0, 1))

        def body(u, carry):
            cur = _row_chunk(u, blk)
            both = pl.ds(pl.multiple_of((u - 1) * blk, blk), 2 * blk)
            unit(u, lambda sl: q_ref[cur, sl], lambda sl: kc_ref[both, sl],
                 lambda sl: vc_ref[both, sl], 1)
            return carry

        lax.fori_loop(1, n_units, body, 0)
    else:
        variant = jnp.where(first_span, 0, 1)
        if dil == 4:
            pick = lambda ref, r, sl: ref[r, :, sl]
            prev_cur = lambda pref, cref, r, sl: jnp.concatenate(
                [pref[r, :, sl], cref[r, :, sl]], axis=0)
        else:
            pick = lambda ref, r, sl: jnp.concatenate([ref[0, r, :, sl], ref[1, r, :, sl]], axis=0)
            prev_cur = lambda pref, cref, r, sl: jnp.concatenate(
                [pref[0, r, :, sl], pref[1, r, :, sl], cref[0, r, :, sl], cref[1, r, :, sl]],
                axis=0)

        per_iter = ATT_HEADS // heads

        def body(it, carry):
            for sub in range(per_iter):
                r = it * per_iter + sub
                unit(r, lambda sl: pick(q_ref, r, sl),
                     lambda sl: prev_cur(kp_ref, kc_ref, r, sl),
                     lambda sl: prev_cur(vp_ref, vc_ref, r, sl), variant)
            return carry

        lax.fori_loop(0, dil // per_iter, body, 0)


def _att_kernel(q_ref, kc_ref, vc_ref, bias_ref, o_ref, lse_ref, kp_ref, vp_ref, *scratch,
                g, heads):
    blk = ATT_BLK
    dil = DILATED_PATTERNS[g][1]
    first_span = pl.program_id(2) == 0

    @pl.when(first_span)
    def _():
        kp_ref[...] = jnp.zeros_like(kp_ref)
        vp_ref[...] = jnp.zeros_like(vp_ref)

    if dil == 1:
        def store(u, h, o, lse_tile):
            rows = _row_chunk(u, blk)
            if h is None:
                lse_ref[rows, :] = lse_tile
            else:
                o_ref[rows, h * HEAD_DIM:(h + 1) * HEAD_DIM] = o.astype(BF16)
    else:
        o_scr, lse_scr = scratch

        def store(u, h, o, lse_tile):
            rows = pl.ds(u, blk, stride=dil)
            if h is None:
                lse_scr[rows, :] = lse_tile
            else:
                o_scr[h, rows, :] = o

    _att_units(q_ref, kc_ref, kp_ref, vc_ref, vp_ref, bias_ref, store, g=g, heads=heads,
               first_span=first_span)

    if dil == 1:
        last = slice(kc_ref.shape[0] - blk, kc_ref.shape[0])
        kp_ref[...] = kc_ref[last, :]
        vp_ref[...] = vc_ref[last, :]
    else:
        kp_ref[...] = kc_ref[...]
        vp_ref[...] = vc_ref[...]
        for h in range(heads):
            o_ref[:, h * HEAD_DIM:(h + 1) * HEAD_DIM] = o_scr[h].astype(BF16)
        lse_ref[...] = lse_scr[...]


def _attention_group(qkv, bias, g, batch, seq):
    dil = DILATED_PATTERNS[g][1]
    blk = ATT_BLK
    span = dil * blk if dil > 1 else 4 * blk
    heads = ATT_HEADS if dil < 16 else 4
    n_hg = ATT_HEADS // heads
    width = heads * HEAD_DIM
    n_cols = qkv.shape[1]
    col0 = lambda which: (g * 3 + which) * (ATT_HEADS // heads)

    if dil == 1:
        view = qkv.reshape(batch, seq, n_cols)
        cur = lambda which: pl.BlockSpec((None, span, width),
                                         lambda b, h, s: (b, s, col0(which) + h))
        prev_shape = (blk, width)
        scratch = []
    elif dil == 4:
        halves = TM // (dil * blk)
        view = qkv.reshape(batch, seq // TM, dil, halves, blk, n_cols)
        cur = lambda which: pl.BlockSpec(
            (None, None, dil, None, blk, width),
            lambda b, h, s: (b, s // halves, 0, s % halves, 0, col0(which) + h))
        prev_shape = (dil, blk, width)
        scratch = [pltpu.VMEM((heads, span, HEAD_DIM), F32), pltpu.VMEM((span, LANES), F32)]
    else:
        tiles = span // TM
        per = TM // dil
        view = qkv.reshape(batch, seq // span, tiles, dil, per, n_cols)
        cur = lambda which: pl.BlockSpec((None, None, tiles, dil, per, width),
                                         lambda b, h, s: (b, s, 0, 0, 0, col0(which) + h))
        prev_shape = (tiles, dil, per, width)
        scratch = [pltpu.VMEM((heads, span, HEAD_DIM), F32), pltpu.VMEM((span, LANES), F32)]

    o, lse = pl.pallas_call(
        functools.partial(_att_kernel, g=g, heads=heads),
        grid=(batch, n_hg, seq // span),
        in_specs=[cur(0), cur(1), cur(2),
                  pl.BlockSpec((None, 2, heads, blk, 2 * blk), lambda b, h, s: (g, 0, h, 0, 0))],
        out_specs=[pl.BlockSpec((None, span, width), lambda b, h, s: (b, s, h)),
                   pl.BlockSpec((None, None, span, LANES), lambda b, h, s: (b, h, s, 0))],
        out_shape=[jax.ShapeDtypeStruct((batch, seq, ATT_HEADS * HEAD_DIM), BF16),
                   jax.ShapeDtypeStruct((batch, n_hg, seq, LANES), F32)],
        scratch_shapes=[pltpu.VMEM(prev_shape, BF16), pltpu.VMEM(prev_shape, BF16)] + scratch,
        compiler_params=_params(3),
        name=f"dilated_attention_g{g}",
    )(view, view, view, bias)
    return o, lse


MERGE_ROWS = 256


def _att_out_kernel(o0_ref, o1_ref, o2_ref, l0_ref, l1_ref, l2_ref, w_ref, x_ref, g_ref,
                    out_ref):
    o_refs = (o0_ref, o1_ref, o2_ref)
    l_refs = (l0_ref, l1_ref, l2_ref)
    tm = x_ref.shape[0]
    lane = lax.broadcasted_iota(jnp.int32, (MERGE_ROWS, LANES), 1)
    for rc in range(tm // MERGE_ROWS):
        rows = slice(rc * MERGE_ROWS, (rc + 1) * MERGE_ROWS)
        lses = []
        for l_ref in l_refs:
            tile = l_ref[0, rows, :]
            for hg in range(1, l_ref.shape[0]):
                tile = tile + l_ref[hg, rows, :]
            lses.append(tile)
        mx = jnp.maximum(jnp.maximum(lses[0], lses[1]), lses[2])
        es = [jnp.exp2(t - mx) for t in lses]
        den = es[0] + es[1] + es[2]
        ws = [e / den for e in es]
        pieces = []
        for h in range(ATT_HEADS):
            sl = slice(h * HEAD_DIM, (h + 1) * HEAD_DIM)
            acc = None
            for w_g, o_ref in zip(ws, o_refs):
                w_col = jnp.sum(jnp.where(lane == h, w_g, 0.0), axis=-1, keepdims=True)
                term = w_col * o_ref[rows, sl].astype(F32)
                acc = term if acc is None else acc + term
            pieces.append(acc.astype(BF16))
        merged = jnp.concatenate(pieces, axis=1)
        y = jnp.dot(merged, w_ref[...], preferred_element_type=F32)
        out_ref[rows, :] = x_ref[rows, :] + g_ref[...] * y


def _att_out_proj(outs, lses, w, x, mod3, mod_row, tiles_per_batch_tm, tm):
    t, n = x.shape
    k = w.shape[0]
    seq_tiles = tiles_per_batch_tm
    o_spec = pl.BlockSpec((tm, k), lambda i, j: (i, 0))
    l_spec = lambda n_hg: pl.BlockSpec((None, n_hg, tm, LANES),
                                       lambda i, j: (i // seq_tiles, 0, i % seq_tiles, 0))
    return pl.pallas_call(
        _att_out_kernel,
        grid=(t // tm, 1),
        in_specs=[o_spec, o_spec, o_spec] + [l_spec(l.shape[1]) for l in lses]
                 + [pl.BlockSpec((k, n), lambda i, j: (0, 0)),
                    pl.BlockSpec((tm, n), lambda i, j: (i, 0)),
                    _gate_spec(n, n, mod_row, 2, seq_tiles)],
        out_specs=pl.BlockSpec((tm, n), lambda i, j: (i, 0)),
        out_shape=jax.ShapeDtypeStruct((t, n), F32),
        compiler_params=_params(2),
        name="att_merge_out_proj",
    )(*[o.reshape(t, k) for o in outs], *lses, w, x, mod3)


RET_TOKENS = 1024


def _retention_kernel(q_ref, k_ref, v_ref, g_ref, cos_ref, sin_ref, inner_ref, cross_ref,
                      sdec_ref, cdec_ref, o_ref, state_ref):
    @pl.when(pl.program_id(2) == 0)
    def _():
        state_ref[...] = jnp.zeros_like(state_ref)

    half = RET_HEAD_QK // 2
    dn_nt = (((1,), (1,)), ((), ()))
    dn_tn = (((0,), (0,)), ((), ()))
    inner = inner_ref[...]
    cross = cross_ref[...]
    sdec = sdec_ref[...]
    cdec = cdec_ref[...]
    for c in range(RET_TOKENS // RET_CHUNK):
        rows = slice(c * RET_CHUNK, (c + 1) * RET_CHUNK)
        cos = cos_ref[rows, :]
        sin = sin_ref[rows, :]
        q = q_ref[rows, :].astype(F32)
        k = k_ref[rows, :].astype(F32)
        q1, q2 = q[:, :half], q[:, half:]
        k1, k2 = k[:, :half], k[:, half:]
        qr1 = q1 * cos - q2 * sin
        qr2 = q1 * sin + q2 * cos
        kr1 = k1 * cos - k2 * sin
        kr2 = k1 * sin + k2 * cos
        q_b = jnp.concatenate([qr1, qr2], axis=1).astype(BF16)
        qc_b = jnp.concatenate([qr1 * cross, qr2 * cross], axis=1).astype(BF16)
        k_b = jnp.concatenate([kr1, kr2], axis=1).astype(BF16)
        ks_b = jnp.concatenate([kr1 * sdec, kr2 * sdec], axis=1).astype(BF16)
        v = v_ref[rows, :]

        scores = lax.dot_general(q_b, k_b, dn_nt, preferred_element_type=F32) * inner
        state = state_ref[...]
        o = (jnp.dot(scores.astype(BF16), v, preferred_element_type=F32)
             + jnp.dot(qc_b, state.astype(BF16), preferred_element_type=F32))
        state_ref[...] = state * cdec + lax.dot_general(ks_b, v, dn_tn,
                                                        preferred_element_type=F32)

        ms = jnp.mean(o * o, axis=-1, keepdims=True)
        gate = g_ref[rows, :].astype(F32)
        o_ref[rows, :] = (gate * (o * lax.rsqrt(ms + NORM_EPS))).astype(BF16)


def _retention(qkvg, batch, seq):
    c = RET_CHUNK
    hq, hv = RET_HEAD_QK, RET_HEAD_V
    n_qk = RET_HEADS * hq
    n_v = RET_HEADS * hv

    half = hq // 2
    inv = 1.0 / (10000.0 ** jnp.linspace(0.0, 1.0, half, dtype=F32))
    ang = jnp.arange(seq, dtype=F32)[:, None] * inv[None, :]
    cos, sin = jnp.cos(ang), jnp.sin(ang)
    log_gamma = jnp.log(1.0 - 2.0 ** (-5.0 - jnp.arange(RET_HEADS, dtype=F32)))
    pos = jnp.arange(c, dtype=F32)
    diff = pos[:, None] - pos[None, :]
    inner = jnp.where(diff[None] >= 0,
                      jnp.exp(jnp.maximum(diff, 0.0)[None] * log_gamma[:, None, None]), 0.0)
    cross = jnp.exp((pos[None, :] + 1.0) * log_gamma[:, None])
    sdec = jnp.exp((c - 1.0 - pos)[None, :] * log_gamma[:, None])
    cdec = jnp.exp(c * log_gamma)
    k_scale = hq ** -0.5
    inner = inner * k_scale
    cross_l = jnp.broadcast_to(cross[:, :, None], (RET_HEADS, c, half))
    sdec_l = jnp.broadcast_to((sdec * k_scale)[:, :, None], (RET_HEADS, c, half))
    cdec_l = jnp.broadcast_to(cdec[:, None, None], (RET_HEADS, 1, hv))

    x = qkvg.reshape(batch, seq, qkvg.shape[1])
    tt = RET_TOKENS
    head_tab = lambda shape: pl.BlockSpec((None,) + shape, lambda b, h, s: (h, 0, 0))
    out = pl.pallas_call(
        _retention_kernel,
        grid=(batch, RET_HEADS, seq // tt),
        in_specs=[pl.BlockSpec((None, tt, hq), lambda b, h, s: (b, s, h)),
                  pl.BlockSpec((None, tt, hq), lambda b, h, s: (b, s, n_qk // hq + h)),
                  pl.BlockSpec((None, tt, hv), lambda b, h, s: (b, s, 2 * n_qk // hv + h)),
                  pl.BlockSpec((None, tt, hv), lambda b, h, s: (b, s, (2 * n_qk + n_v) // hv + h)),
                  pl.BlockSpec((tt, half), lambda b, h, s: (s, 0)),
                  pl.BlockSpec((tt, half), lambda b, h, s: (s, 0)),
                  head_tab((c, c)), head_tab((c, half)), head_tab((c, half)), head_tab((1, hv))],
        out_specs=pl.BlockSpec((None, tt, hv), lambda b, h, s: (b, s, h)),
        out_shape=jax.ShapeDtypeStruct((batch, seq, n_v), BF16),
        scratch_shapes=[pltpu.VMEM((hq, hv), F32)],
        compiler_params=_params(3),
        name="retention",
    )(x, x, x, x, cos, sin, inner, cross_l, sdec_l, cdec_l)
    return out.reshape(batch * seq, n_v)


def kernel(x, c, w_mod, b_mod, norm_mix, norm_ffn, rel_bias, att_w_qkv, att_q_gain, att_k_gain,
           att_w_o, ret_w_qkvg, ret_w_o, ffn_w_up, ffn_w_down):
    batch, seq, d = x.shape
    depth = w_mod.shape[0]
    assert seq % (2 * TM) == 0 and d == ATT_HEADS * HEAD_DIM
    tiles_per_batch = seq // TM

    mod = _modulation(c, w_mod, b_mod)
    mod3 = mod.reshape(depth * batch, 1, 6 * d)
    bias = _bias_tiles(rel_bias)
    xf = x.reshape(batch * seq, d)

    for i in range(depth):
        jdx = i // 2
        mod_row = i * batch
        gain_mix = norm_mix[i].reshape(1, d)
        if i % 2 == 0:
            scale = HEAD_DIM ** -0.5 * LOG2E
            ones = jnp.ones((ATT_HEADS * HEAD_DIM,), F32)
            head_gain = jnp.concatenate(
                [jnp.concatenate([jnp.tile(att_q_gain[jdx, g] * scale, ATT_HEADS),
                                  jnp.tile(att_k_gain[jdx, g], ATT_HEADS), ones])
                 for g in range(N_GROUPS)]).reshape(1, -1)
            qkv = _att_proj(xf, gain_mix, mod3, mod_row, att_w_qkv[jdx].astype(BF16), head_gain,
                            tiles_per_batch)
            outs, lses = zip(*[_attention_group(qkv, bias, g, batch, seq)
                               for g in range(N_GROUPS)])
            att_tm = 512
            xf = _att_out_proj(outs, lses, att_w_o[jdx].astype(BF16), xf, mod3, mod_row,
                               seq // att_tm, att_tm)
        else:
            qkvg = _proj(xf, gain_mix, mod3, mod_row, ret_w_qkvg[jdx].astype(BF16),
                         tiles_per_batch, 2 * RET_HEADS * RET_HEAD_QK + RET_HEADS * RET_HEAD_V)
            mixed = _retention(qkvg, batch, seq)
            xf = _out_proj(mixed, ret_w_o[jdx].astype(BF16), xf, mod3, mod_row, 2,
                           tiles_per_batch, TM, TN, "ret_out_proj")

        act = _ffn_up(xf, norm_ffn[i].reshape(1, d), mod3, mod_row, ffn_w_up[i].astype(BF16),
                      tiles_per_batch)
        xf = _out_proj(act, ffn_w_down[i].astype(BF16), xf, mod3, mod_row, 5, tiles_per_batch,
                       TM, 512, "ffn_down")
    return xf.reshape(batch, seq, d)
```

```python
import functools
import math

import numpy as np
import jax
import jax.numpy as jnp
from jax import lax
from jax.experimental import pallas as pl
from jax.experimental.pallas import tpu as pltpu

F32 = jnp.float32
BF16 = jnp.bfloat16

NORM_EPS = 1e-6
MASK_VALUE = -1e30
LOG2E = math.log2(math.e)
LANES = 128

ATT_HEADS = 16
HEAD_DIM = 128
DILATED_PATTERNS = ((128, 1), (512, 4), (2048, 16))
N_GROUPS = len(DILATED_PATTERNS)
ATT_BLK = 128
REL_BUCKETS = 32
REL_MAX_DIST = 2048

RET_HEADS = 8
RET_HEAD_QK = 256
RET_HEAD_V = 512
RET_CHUNK = 128

VMEM_LIMIT_BYTES = 56 * 1024 * 1024

TM = 1024
TN = 1024
TM_WIDE = 2048
TN_WIDE = 2048
ROW_CHUNK = 256
MM_ROWS = 256
STAGE_COLS = 4


def _params(n_axes):
    return pltpu.CompilerParams(dimension_semantics=("arbitrary",) * n_axes,
                                vmem_limit_bytes=VMEM_LIMIT_BYTES)


def _silu(v):
    return v / (1.0 + jnp.exp(-v))


def _row_chunk(i, size):
    if isinstance(i, int):
        return pl.ds(i * size, size)
    return pl.ds(pl.multiple_of(i * size, size), size)


def _mod_kernel(c_ref, w_ref, b_ref, o_ref):
    cond = _silu(c_ref[...]).astype(BF16)
    o_ref[...] = (jnp.dot(cond, w_ref[...].astype(BF16), preferred_element_type=F32)
                  + b_ref[...])


def _modulation(c, w_mod, b_mod):
    depth, d, n = w_mod.shape
    b = c.shape[0]
    tn = 1024
    return pl.pallas_call(
        _mod_kernel,
        grid=(depth, n // tn),
        in_specs=[pl.BlockSpec((b, d), lambda l, j: (0, 0)),
                  pl.BlockSpec((None, d, tn), lambda l, j: (l, 0, j)),
                  pl.BlockSpec((None, 1, tn), lambda l, j: (l, 0, j))],
        out_specs=pl.BlockSpec((None, b, tn), lambda l, j: (l, 0, j)),
        out_shape=jax.ShapeDtypeStruct((depth, b, n), F32),
        compiler_params=_params(2),
        name="adaln_mod",
    )(c, w_mod, b_mod.reshape(depth, 1, n))


def _row_rsqrt_to_scratch(x_ref, rs_ref):
    def body(i, carry):
        rows = _row_chunk(i, ROW_CHUNK)
        xx = x_ref[rows, :]
        ms = jnp.mean(xx * xx, axis=-1, keepdims=True)
        rs_ref[rows, :] = jnp.broadcast_to(lax.rsqrt(ms + NORM_EPS), (ROW_CHUNK, LANES))
        return carry

    lax.fori_loop(0, x_ref.shape[0] // ROW_CHUNK, body, 0)


def _norm_mod_to_scratch(x_ref, gain_ref, sc_ref, sh_ref, h_ref, rs_ref):
    _row_rsqrt_to_scratch(x_ref, rs_ref)

    def body(i, carry):
        rows = _row_chunk(i, ROW_CHUNK)
        rs = rs_ref[rows, :]
        for cb in range(x_ref.shape[1] // LANES):
            cols = slice(cb * LANES, (cb + 1) * LANES)
            hn = ((x_ref[rows, cols] * rs * gain_ref[:, cols]) * (1.0 + sc_ref[:, cols])
                  + sh_ref[:, cols])
            h_ref[rows, cols] = hn.astype(BF16)
        return carry

    lax.fori_loop(0, x_ref.shape[0] // ROW_CHUNK, body, 0)


def _x_tile_spec(tm, d):
    return pl.BlockSpec((tm, d), lambda i, j: (i, 0), pipeline_mode=pl.Buffered(1))


def _mod_spec(d, tiles_per_batch, row, chunk):
    return pl.BlockSpec((None, 1, d), lambda i, j: (row + i // tiles_per_batch, 0, chunk))


def _att_prologue(x_ref, gain_ref, sc_ref, sh_ref, h_ref, rs_ref, stage_ref, stage2_ref):
    tm, d = x_ref.shape
    n_chunks = tm // ROW_CHUNK

    _row_rsqrt_to_scratch(x_ref, rs_ref)

    for cg in range(d // (STAGE_COLS * LANES)):
        col = lambda cc: slice((cg * STAGE_COLS + cc) * LANES, (cg * STAGE_COLS + cc + 1) * LANES)

        def natural(i, carry):
            rows = _row_chunk(i, ROW_CHUNK)
            rs = rs_ref[rows, :]
            for cc in range(STAGE_COLS):
                cols = col(cc)
                hn = ((x_ref[rows, cols] * rs * gain_ref[:, cols]) * (1.0 + sc_ref[:, cols])
                      + sh_ref[:, cols])
                h_ref[0, rows, cols] = hn.astype(BF16)
                stage_ref[cc, rows, :] = hn
            return carry

        lax.fori_loop(0, n_chunks, natural, 0)

        dil = DILATED_PATTERNS[1][1]
        per = tm // dil
        sub = DILATED_PATTERNS[2][1] // dil
        per16 = per // sub

        def permute4(r, carry):
            dst = _row_chunk(r, per)
            for cc in range(STAGE_COLS):
                v = stage_ref[cc, pl.ds(r, per, stride=dil), :]
                h_ref[1, dst, col(cc)] = v.astype(BF16)
                stage2_ref[cc, dst, :] = v
            return carry

        lax.fori_loop(0, dil, permute4, 0)

        def permute16(r, carry):
            src = pl.ds((r % dil) * per + r // dil, per16, stride=sub)
            dst = _row_chunk(r, per16)
            for cc in range(STAGE_COLS):
                h_ref[2, dst, col(cc)] = stage2_ref[cc, src, :].astype(BF16)
            return carry

        lax.fori_loop(0, dil * sub, permute16, 0)


def _att_proj_kernel(x_ref, gain_ref, sc_ref, sh_ref, w_ref, hg_ref, o_ref, h_ref, rs_ref,
                     stage_ref, stage2_ref, *, d_model):
    j = pl.program_id(1)

    @pl.when(j == 0)
    def _():
        _att_prologue(x_ref, gain_ref, sc_ref, sh_ref, h_ref, rs_ref, stage_ref, stage2_ref)

    tn = w_ref.shape[1]
    group = (j * tn) // (3 * d_model)
    is_v = ((j * tn) // d_model) % 3 == 2

    def tile(head_norm):
        for rc in range(h_ref.shape[1] // MM_ROWS):
            rows = slice(rc * MM_ROWS, (rc + 1) * MM_ROWS)
            acc = jnp.dot(h_ref[group, rows, :], w_ref[...], preferred_element_type=F32)
            if not head_norm:
                o_ref[rows, :] = acc.astype(BF16)
                continue
            for hh in range(tn // HEAD_DIM):
                sl = slice(hh * HEAD_DIM, (hh + 1) * HEAD_DIM)
                a = acc[:, sl]
                ms = jnp.mean(a * a, axis=-1, keepdims=True)
                o_ref[rows, sl] = (a * lax.rsqrt(ms + NORM_EPS) * hg_ref[:, sl]).astype(BF16)

    pl.when(is_v)(functools.partial(tile, False))
    pl.when(jnp.logical_not(is_v))(functools.partial(tile, True))


def _att_proj(x, gain, mod3, mod_row, w, head_gain, tiles_per_batch):
    t, d = x.shape
    n = w.shape[1]
    return pl.pallas_call(
        functools.partial(_att_proj_kernel, d_model=d),
        grid=(t // TM, n // TN_WIDE),
        in_specs=[_x_tile_spec(TM, d),
                  pl.BlockSpec((1, d), lambda i, j: (0, 0)),
                  _mod_spec(d, tiles_per_batch, mod_row, 1),
                  _mod_spec(d, tiles_per_batch, mod_row, 0),
                  pl.BlockSpec((d, TN_WIDE), lambda i, j: (0, j)),
                  pl.BlockSpec((1, TN_WIDE), lambda i, j: (0, j))],
        out_specs=pl.BlockSpec((TM, TN_WIDE), lambda i, j: (i, j)),
        out_shape=jax.ShapeDtypeStruct((t, n), BF16),
        scratch_shapes=[pltpu.VMEM((N_GROUPS, TM, d), BF16),
                        pltpu.VMEM((TM, LANES), F32),
                        pltpu.VMEM((STAGE_COLS, TM, LANES), F32),
                        pltpu.VMEM((STAGE_COLS, TM, LANES), F32)],
        compiler_params=_params(2),
        name="att_qkv_proj",
    )(x, gain, mod3, mod3, w, head_gain)


def _proj_kernel(x_ref, gain_ref, sc_ref, sh_ref, w_ref, o_ref, h_ref, rs_ref, *, gate_col0):
    j = pl.program_id(1)

    @pl.when(j == 0)
    def _():
        _norm_mod_to_scratch(x_ref, gain_ref, sc_ref, sh_ref, h_ref, rs_ref)

    is_gate = j * w_ref.shape[1] >= gate_col0

    def tile(gate):
        for rc in range(h_ref.shape[0] // MM_ROWS):
            rows = slice(rc * MM_ROWS, (rc + 1) * MM_ROWS)
            acc = jnp.dot(h_ref[rows, :], w_ref[...], preferred_element_type=F32)
            o_ref[rows, :] = (_silu(acc) if gate else acc).astype(BF16)

    pl.when(is_gate)(functools.partial(tile, True))
    pl.when(jnp.logical_not(is_gate))(functools.partial(tile, False))


def _proj(x, gain, mod3, mod_row, w, seq, gate_col0):
    t, d = x.shape
    n = w.shape[1]
    tm = TM_WIDE
    tiles_per_batch = seq // tm
    return pl.pallas_call(
        functools.partial(_proj_kernel, gate_col0=gate_col0),
        grid=(t // tm, n // TN),
        in_specs=[_x_tile_spec(tm, d),
                  pl.BlockSpec((1, d), lambda i, j: (0, 0)),
                  _mod_spec(d, tiles_per_batch, mod_row, 1),
                  _mod_spec(d, tiles_per_batch, mod_row, 0),
                  pl.BlockSpec((d, TN), lambda i, j: (0, j))],
        out_specs=pl.BlockSpec((tm, TN), lambda i, j: (i, j)),
        out_shape=jax.ShapeDtypeStruct((t, n), BF16),
        scratch_shapes=[pltpu.VMEM((tm, d), BF16), pltpu.VMEM((tm, LANES), F32)],
        compiler_params=_params(2),
        name="ret_qkvg_proj",
    )(x, gain, mod3, mod3, w)


def _ffn_up_kernel(x_ref, gain_ref, sc_ref, sh_ref, wg_ref, wu_ref, o_ref, h_ref, rs_ref):
    @pl.when(pl.program_id(1) == 0)
    def _():
        _norm_mod_to_scratch(x_ref, gain_ref, sc_ref, sh_ref, h_ref, rs_ref)

    for rc in range(h_ref.shape[0] // MM_ROWS):
        rows = slice(rc * MM_ROWS, (rc + 1) * MM_ROWS)
        h = h_ref[rows, :]
        gate = jnp.dot(h, wg_ref[...], preferred_element_type=F32)
        up = jnp.dot(h, wu_ref[...], preferred_element_type=F32)
        o_ref[rows, :] = (_silu(gate) * up).astype(BF16)


def _ffn_up(x, gain, mod3, mod_row, w_up, seq):
    t, d = x.shape
    d_ff = w_up.shape[1] // 2
    tf = 512
    n_f = d_ff // tf
    tm = TM_WIDE
    tiles_per_batch = seq // tm
    return pl.pallas_call(
        _ffn_up_kernel,
        grid=(t // tm, n_f),
        in_specs=[_x_tile_spec(tm, d),
                  pl.BlockSpec((1, d), lambda i, j: (0, 0)),
                  _mod_spec(d, tiles_per_batch, mod_row, 4),
                  _mod_spec(d, tiles_per_batch, mod_row, 3),
                  pl.BlockSpec((d, tf), lambda i, j: (0, j)),
                  pl.BlockSpec((d, tf), lambda i, j: (0, j + n_f))],
        out_specs=pl.BlockSpec((tm, tf), lambda i, j: (i, j)),
        out_shape=jax.ShapeDtypeStruct((t, d_ff), BF16),
        scratch_shapes=[pltpu.VMEM((tm, d), BF16), pltpu.VMEM((tm, LANES), F32)],
        compiler_params=_params(2),
        name="ffn_up",
    )(x, gain, mod3, mod3, w_up, w_up)


def _out_proj_kernel(a_ref, w_ref, x_ref, g_ref, o_ref):
    for rc in range(a_ref.shape[0] // MM_ROWS):
        rows = slice(rc * MM_ROWS, (rc + 1) * MM_ROWS)
        y = jnp.dot(a_ref[rows, :], w_ref[...], preferred_element_type=F32)
        o_ref[rows, :] = x_ref[rows, :] + g_ref[...] * y


def _gate_spec(n, tn, mod_row, gate_chunk, tiles_per_batch):
    return pl.BlockSpec((None, 1, tn),
                        lambda i, j: (mod_row + i // tiles_per_batch, 0,
                                      gate_chunk * (n // tn) + j))


def _out_proj(a, w, x, mod3, mod_row, gate_chunk, tiles_per_batch, tm, tn, name):
    t, k = a.shape
    n = w.shape[1]
    return pl.pallas_call(
        _out_proj_kernel,
        grid=(t // tm, n // tn),
        in_specs=[pl.BlockSpec((tm, k), lambda i, j: (i, 0)),
                  pl.BlockSpec((k, tn), lambda i, j: (0, j)),
                  pl.BlockSpec((tm, tn), lambda i, j: (i, j)),
                  _gate_spec(n, tn, mod_row, gate_chunk, tiles_per_batch)],
        out_specs=pl.BlockSpec((tm, tn), lambda i, j: (i, j)),
        out_shape=jax.ShapeDtypeStruct((t, n), F32),
        compiler_params=_params(2),
        name=name,
    )(a, w, x, mod3)


def _bias_kernel(tab_ref, bucket_ref, valid_ref, o_ref):
    g = pl.program_id(0)
    h = pl.program_id(1)
    bucket = bucket_ref[...]
    col = g * ATT_HEADS + h
    acc = jnp.zeros(bucket.shape, F32)
    for b in range(REL_BUCKETS):
        acc = jnp.where(bucket == b, tab_ref[b, col] * LOG2E, acc)
    for first in range(2):
        o_ref[first] = jnp.where(valid_ref[first] != 0, acc, MASK_VALUE)


def _t5_bucket_np(dist):
    max_exact = REL_BUCKETS // 2
    d_f = np.maximum(dist, 1).astype(np.float32)
    large = max_exact + (np.log(d_f / np.float32(max_exact))
                         / np.float32(math.log(REL_MAX_DIST / max_exact))
                         * np.float32(REL_BUCKETS - max_exact)).astype(np.int32)
    large = np.minimum(large, REL_BUCKETS - 1)
    return np.where(dist < max_exact, dist, large).astype(np.int32)


def _bias_tiles(rel_bias):
    blk = ATT_BLK
    qi = np.arange(blk)[:, None]
    ki = np.arange(2 * blk)[None, :]
    delta = blk + qi - ki
    band = (delta >= 0) & (delta <= blk)
    valid = np.stack([band & (ki >= blk), band]).astype(np.int32)
    buckets = np.stack([_t5_bucket_np(np.maximum(delta, 0) * dil)
                        for _, dil in DILATED_PATTERNS])
    return pl.pallas_call(
        _bias_kernel,
        grid=(N_GROUPS, ATT_HEADS),
        in_specs=[pl.BlockSpec(memory_space=pltpu.SMEM),
                  pl.BlockSpec((None, blk, 2 * blk), lambda g, h: (g, 0, 0)),
                  pl.BlockSpec((2, blk, 2 * blk), lambda g, h: (0, 0, 0))],
        out_specs=pl.BlockSpec((None, 2, None, blk, 2 * blk), lambda g, h: (g, 0, h, 0, 0)),
        out_shape=jax.ShapeDtypeStruct((N_GROUPS, 2, ATT_HEADS, blk, 2 * blk), F32),
        compiler_params=_params(2),
        name="rel_bias_tiles",
    )(rel_bias, jnp.asarray(buckets), jnp.asarray(valid))


ATT_HEAD_BATCH = 4


def _att_units(q_ref, kc_ref, kp_ref, vc_ref, vp_ref, bias_ref, store, *, g, heads, first_span):
    blk = ATT_BLK
    dil = DILATED_PATTERNS[g][1]
    dn = (((1,), (1,)), ((), ()))
    ones = jnp.ones((2 * blk, HEAD_DIM), BF16)
    lane = lax.broadcasted_iota(jnp.int32, (blk, LANES), 1)
    head0 = pl.program_id(1) * heads

    def unit(u, q_of, k_of, v_of, variant):
        m_tile = jnp.zeros((blk, LANES), F32)
        l_tile = jnp.ones((blk, LANES), F32)
        for hb in range(heads // ATT_HEAD_BATCH):
            hs = range(hb * ATT_HEAD_BATCH, (hb + 1) * ATT_HEAD_BATCH)
            sl = lambda h: slice(h * HEAD_DIM, (h + 1) * HEAD_DIM)
            s = jnp.concatenate(
                [lax.dot_general(q_of(sl(h)), k_of(sl(h)), dn, preferred_element_type=F32)
                 for h in hs], axis=0)
            s = s + bias_ref[variant, hs.start:hs.stop].reshape(ATT_HEAD_BATCH * blk, 2 * blk)
            m = jnp.max(jnp.maximum(s[:, :blk], s[:, blk:]), axis=-1, keepdims=True)
            p = jnp.exp2(s - m).astype(BF16)
            for idx, h in enumerate(hs):
                rows = slice(idx * blk, (idx + 1) * blk)
                v_ext = jnp.concatenate([v_of(sl(h)), ones], axis=1)
                pv = jnp.dot(p[rows], v_ext, preferred_element_type=F32)
                denom = pv[:, HEAD_DIM:]
                o = pv[:, :HEAD_DIM] / denom
                mine = lane == head0 + h
                m_tile = jnp.where(mine, m[rows], m_tile)
                l_tile = jnp.where(mine, denom, l_tile)
                store(u, h, o, None)
        store(u, None, None, m_tile + jnp.log2(l_tile))

    if dil == 1:
        n_units = q_ref.shape[0] // blk
        unit(0,
             lambda sl: q_ref[0:blk, sl],
             lambda sl: jnp.concatenate([kp_ref[:, sl], kc_ref[0:blk, sl]], axis=0),
             lambda sl: jnp.concatenate([vp_ref[:, sl], vc_ref[0:blk, sl]], axis=0),
             jnp.where(first_span, 0, 1))

        def body(u, carry):
            cur = _row_chunk(u, blk)
            both = pl.ds(pl.multiple_of((u - 1) * blk, blk), 2 * blk)
            unit(u, lambda sl: q_ref[cur, sl], lambda sl: kc_ref[both, sl],
                 lambda sl: vc_ref[both, sl], 1)
            return carry

        lax.fori_loop(1, n_units, body, 0)
    else:
        variant = jnp.where(first_span, 0, 1)
        if dil == 4:
            pick = lambda ref, r, sl: ref[r, :, sl]
            prev_cur = lambda pref, cref, r, sl: jnp.concatenate(
                [pref[r, :, sl], cref[r, :, sl]], axis=0)
        else:
            pick = lambda ref, r, sl: jnp.concatenate([ref[0, r, :, sl], ref[1, r, :, sl]], axis=0)
            prev_cur = lambda pref, cref, r, sl: jnp.concatenate(
                [pref[0, r, :, sl], pref[1, r, :, sl], cref[0, r, :, sl], cref[1, r, :, sl]],
                axis=0)

        per_iter = ATT_HEADS // heads

        def body(it, carry):
            for sub in range(per_iter):
                r = it * per_iter + sub
                unit(r, lambda sl: pick(q_ref, r, sl),
                     lambda sl: prev_cur(kp_ref, kc_ref, r, sl),
                     lambda sl: prev_cur(vp_ref, vc_ref, r, sl), variant)
            return carry

        lax.fori_loop(0, dil // per_iter, body, 0)


def _att_kernel(q_ref, kc_ref, vc_ref, bias_ref, o_ref, lse_ref, kp_ref, vp_ref, *scratch,
                g, heads):
    blk = ATT_BLK
    dil = DILATED_PATTERNS[g][1]
    first_span = pl.program_id(2) == 0

    @pl.when(first_span)
    def _():
        kp_ref[...] = jnp.zeros_like(kp_ref)
        vp_ref[...] = jnp.zeros_like(vp_ref)

    if dil == 1:
        def store(u, h, o, lse_tile):
            rows = _row_chunk(u, blk)
            if h is None:
                lse_ref[rows, :] = lse_tile
            else:
                o_ref[rows, h * HEAD_DIM:(h + 1) * HEAD_DIM] = o.astype(BF16)
    else:
        o_scr, lse_scr = scratch

        def store(u, h, o, lse_tile):
            rows = pl.ds(u, blk, stride=dil)
            if h is None:
                lse_scr[rows, :] = lse_tile
            else:
                o_scr[h, rows, :] = o

    _att_units(q_ref, kc_ref, kp_ref, vc_ref, vp_ref, bias_ref, store, g=g, heads=heads,
               first_span=first_span)

    if dil == 1:
        last = slice(kc_ref.shape[0] - blk, kc_ref.shape[0])
        kp_ref[...] = kc_ref[last, :]
        vp_ref[...] = vc_ref[last, :]
    else:
        kp_ref[...] = kc_ref[...]
        vp_ref[...] = vc_ref[...]
        for h in range(heads):
            o_ref[:, h * HEAD_DIM:(h + 1) * HEAD_DIM] = o_scr[h].astype(BF16)
        lse_ref[...] = lse_scr[...]


def _attention_group(qkv, bias, g, batch, seq):
    dil = DILATED_PATTERNS[g][1]
    blk = ATT_BLK
    span = dil * blk if dil > 1 else 4 * blk
    heads = ATT_HEADS if dil < 16 else 4
    n_hg = ATT_HEADS // heads
    width = heads * HEAD_DIM
    n_cols = qkv.shape[1]
    col0 = lambda which: (g * 3 + which) * (ATT_HEADS // heads)

    if dil == 1:
        view = qkv.reshape(batch, seq, n_cols)
        cur = lambda which: pl.BlockSpec((None, span, width),
                                         lambda b, h, s: (b, s, col0(which) + h))
        prev_shape = (blk, width)
        scratch = []
    elif dil == 4:
        halves = TM // (dil * blk)
        view = qkv.reshape(batch, seq // TM, dil, halves, blk, n_cols)
        cur = lambda which: pl.BlockSpec(
            (None, None, dil, None, blk, width),
            lambda b, h, s: (b, s // halves, 0, s % halves, 0, col0(which) + h))
        prev_shape = (dil, blk, width)
        scratch = [pltpu.VMEM((heads, span, HEAD_DIM), F32), pltpu.VMEM((span, LANES), F32)]
    else:
        tiles = span // TM
        per = TM // dil
        view = qkv.reshape(batch, seq // span, tiles, dil, per, n_cols)
        cur = lambda which: pl.BlockSpec((None, None, tiles, dil, per, width),
                                         lambda b, h, s: (b, s, 0, 0, 0, col0(which) + h))
        prev_shape = (tiles, dil, per, width)
        scratch = [pltpu.VMEM((heads, span, HEAD_DIM), F32), pltpu.VMEM((span, LANES), F32)]

    o, lse = pl.pallas_call(
        functools.partial(_att_kernel, g=g, heads=heads),
        grid=(batch, n_hg, seq // span),
        in_specs=[cur(0), cur(1), cur(2),
                  pl.BlockSpec((None, 2, heads, blk, 2 * blk), lambda b, h, s: (g, 0, h, 0, 0))],
        out_specs=[pl.BlockSpec((None, span, width), lambda b, h, s: (b, s, h)),
                   pl.BlockSpec((None, None, span, LANES), lambda b, h, s: (b, h, s, 0))],
        out_shape=[jax.ShapeDtypeStruct((batch, seq, ATT_HEADS * HEAD_DIM), BF16),
                   jax.ShapeDtypeStruct((batch, n_hg, seq, LANES), F32)],
        scratch_shapes=[pltpu.VMEM(prev_shape, BF16), pltpu.VMEM(prev_shape, BF16)] + scratch,
        compiler_params=_params(3),
        name=f"dilated_attention_g{g}",
    )(view, view, view, bias)
    return o, lse


MERGE_ROWS = 256


def _att_out_kernel(o0_ref, o1_ref, o2_ref, l0_ref, l1_ref, l2_ref, w_ref, x_ref, g_ref,
                    out_ref):
    o_refs = (o0_ref, o1_ref, o2_ref)
    l_refs = (l0_ref, l1_ref, l2_ref)
    tm = x_ref.shape[0]
    lane = lax.broadcasted_iota(jnp.int32, (MERGE_ROWS, LANES), 1)
    for rc in range(tm // MERGE_ROWS):
        rows = slice(rc * MERGE_ROWS, (rc + 1) * MERGE_ROWS)
        lses = []
        for l_ref in l_refs:
            tile = l_ref[0, rows, :]
            for hg in range(1, l_ref.shape[0]):
                tile = tile + l_ref[hg, rows, :]
            lses.append(tile)
        mx = jnp.maximum(jnp.maximum(lses[0], lses[1]), lses[2])
        es = [jnp.exp2(t - mx) for t in lses]
        den = es[0] + es[1] + es[2]
        ws = [e / den for e in es]
        pieces = []
        for h in range(ATT_HEADS):
            sl = slice(h * HEAD_DIM, (h + 1) * HEAD_DIM)
            acc = None
            for w_g, o_ref in zip(ws, o_refs):
                w_col = jnp.sum(jnp.where(lane == h, w_g, 0.0), axis=-1, keepdims=True)
                term = w_col * o_ref[rows, sl].astype(F32)
                acc = term if acc is None else acc + term
            pieces.append(acc.astype(BF16))
        merged = jnp.concatenate(pieces, axis=1)
        y = jnp.dot(merged, w_ref[...], preferred_element_type=F32)
        out_ref[rows, :] = x_ref[rows, :] + g_ref[...] * y


def _att_out_proj(outs, lses, w, x, mod3, mod_row, tiles_per_batch_tm, tm):
    t, n = x.shape
    k = w.shape[0]
    seq_tiles = tiles_per_batch_tm
    o_spec = pl.BlockSpec((tm, k), lambda i, j: (i, 0))
    l_spec = lambda n_hg: pl.BlockSpec((None, n_hg, tm, LANES),
                                       lambda i, j: (i // seq_tiles, 0, i % seq_tiles, 0))
    return pl.pallas_call(
        _att_out_kernel,
        grid=(t // tm, 1),
        in_specs=[o_spec, o_spec, o_spec] + [l_spec(l.shape[1]) for l in lses]
                 + [pl.BlockSpec((k, n), lambda i, j: (0, 0)),
                    pl.BlockSpec((tm, n), lambda i, j: (i, 0)),
                    _gate_spec(n, n, mod_row, 2, seq_tiles)],
        out_specs=pl.BlockSpec((tm, n), lambda i, j: (i, 0)),
        out_shape=jax.ShapeDtypeStruct((t, n), F32),
        compiler_params=_params(2),
        name="att_merge_out_proj",
    )(*[o.reshape(t, k) for o in outs], *lses, w, x, mod3)


RET_TOKENS = 1024
RET_HEADS_PER_STEP = 2


def _retention_kernel(q_ref, k_ref, v_ref, g_ref, cos_ref, sin_ref, inner_ref, cross_ref,
                      sdec_ref, cdec_ref, o_ref, state_ref):
    @pl.when(pl.program_id(2) == 0)
    def _():
        state_ref[...] = jnp.zeros_like(state_ref)

    half = RET_HEAD_QK // 2
    dn_nt = (((1,), (1,)), ((), ()))
    dn_tn = (((0,), (0,)), ((), ()))
    for c in range(RET_TOKENS // RET_CHUNK):
        rows = slice(c * RET_CHUNK, (c + 1) * RET_CHUNK)
        cos = cos_ref[rows, :]
        sin = sin_ref[rows, :]
        for hh in range(RET_HEADS_PER_STEP):
            qk_cols = lambda part: slice(hh * RET_HEAD_QK + part * half,
                                         hh * RET_HEAD_QK + (part + 1) * half)
            v_cols = slice(hh * RET_HEAD_V, (hh + 1) * RET_HEAD_V)
            q1 = q_ref[rows, qk_cols(0)].astype(F32)
            q2 = q_ref[rows, qk_cols(1)].astype(F32)
            k1 = k_ref[rows, qk_cols(0)].astype(F32)
            k2 = k_ref[rows, qk_cols(1)].astype(F32)
            cross = cross_ref[hh]
            sdec = sdec_ref[hh]
            qr1 = q1 * cos - q2 * sin
            qr2 = q1 * sin + q2 * cos
            kr1 = k1 * cos - k2 * sin
            kr2 = k1 * sin + k2 * cos
            q_b = jnp.concatenate([qr1, qr2], axis=1).astype(BF16)
            qc_b = jnp.concatenate([qr1 * cross, qr2 * cross], axis=1).astype(BF16)
            k_b = jnp.concatenate([kr1, kr2], axis=1).astype(BF16)
            ks_b = jnp.concatenate([kr1 * sdec, kr2 * sdec], axis=1).astype(BF16)
            v = v_ref[rows, v_cols]

            scores = (lax.dot_general(q_b, k_b, dn_nt, preferred_element_type=F32)
                      * inner_ref[hh])
            state = state_ref[hh]
            o = (jnp.dot(scores.astype(BF16), v, preferred_element_type=F32)
                 + jnp.dot(qc_b, state.astype(BF16), preferred_element_type=F32))
            state_ref[hh] = state * cdec_ref[hh] + lax.dot_general(
                ks_b, v, dn_tn, preferred_element_type=F32)

            ms = jnp.mean(o * o, axis=-1, keepdims=True)
            gate = g_ref[rows, v_cols].astype(F32)
            o_ref[rows, v_cols] = (gate * (o * lax.rsqrt(ms + NORM_EPS))).astype(BF16)


def _retention(qkvg, batch, seq):
    c = RET_CHUNK
    hq, hv = RET_HEAD_QK, RET_HEAD_V
    hps = RET_HEADS_PER_STEP
    n_qk = RET_HEADS * hq
    n_v = RET_HEADS * hv

    half = hq // 2
    inv = 1.0 / (10000.0 ** jnp.linspace(0.0, 1.0, half, dtype=F32))
    ang = jnp.arange(seq, dtype=F32)[:, None] * inv[None, :]
    cos, sin = jnp.cos(ang), jnp.sin(ang)
    log_gamma = jnp.log(1.0 - 2.0 ** (-5.0 - jnp.arange(RET_HEADS, dtype=F32)))
    pos = jnp.arange(c, dtype=F32)
    diff = pos[:, None] - pos[None, :]
    inner = jnp.where(diff[None] >= 0,
                      jnp.exp(jnp.maximum(diff, 0.0)[None] * log_gamma[:, None, None]), 0.0)
    cross = jnp.exp((pos[None, :] + 1.0) * log_gamma[:, None])
    sdec = jnp.exp((c - 1.0 - pos)[None, :] * log_gamma[:, None])
    cdec = jnp.exp(c * log_gamma)
    k_scale = hq ** -0.5
    inner = inner * k_scale
    cross_l = jnp.broadcast_to(cross[:, :, None], (RET_HEADS, c, half))
    sdec_l = jnp.broadcast_to((sdec * k_scale)[:, :, None], (RET_HEADS, c, half))
    cdec_l = jnp.broadcast_to(cdec[:, None, None], (RET_HEADS, 1, hv))

    x = qkvg.reshape(batch, seq, qkvg.shape[1])
    tt = RET_TOKENS
    wq, wv = hps * hq, hps * hv
    head_tab = lambda shape: pl.BlockSpec((hps,) + shape, lambda b, h, s: (h, 0, 0))
    out = pl.pallas_call(
        _retention_kernel,
        grid=(batch, RET_HEADS // hps, seq // tt),
        in_specs=[pl.BlockSpec((None, tt, wq), lambda b, h, s: (b, s, h)),
                  pl.BlockSpec((None, tt, wq), lambda b, h, s: (b, s, n_qk // wq + h)),
                  pl.BlockSpec((None, tt, wv), lambda b, h, s: (b, s, 2 * n_qk // wv + h)),
                  pl.BlockSpec((None, tt, wv), lambda b, h, s: (b, s, (2 * n_qk + n_v) // wv + h)),
                  pl.BlockSpec((tt, half), lambda b, h, s: (s, 0)),
                  pl.BlockSpec((tt, half), lambda b, h, s: (s, 0)),
                  head_tab((c, c)), head_tab((c, half)), head_tab((c, half)), head_tab((1, hv))],
        out_specs=pl.BlockSpec((None, tt, wv), lambda b, h, s: (b, s, h)),
        out_shape=jax.ShapeDtypeStruct((batch, seq, n_v), BF16),
        scratch_shapes=[pltpu.VMEM((hps, hq, hv), F32)],
        compiler_params=_params(3),
        name="retention",
    )(x, x, x, x, cos, sin, inner, cross_l, sdec_l, cdec_l)
    return out.reshape(batch * seq, n_v)


def kernel(x, c, w_mod, b_mod, norm_mix, norm_ffn, rel_bias, att_w_qkv, att_q_gain, att_k_gain,
           att_w_o, ret_w_qkvg, ret_w_o, ffn_w_up, ffn_w_down):
    batch, seq, d = x.shape
    depth = w_mod.shape[0]
    assert seq % (2 * TM) == 0 and d == ATT_HEADS * HEAD_DIM
    tiles_per_batch = seq // TM

    mod = _modulation(c, w_mod, b_mod)
    mod3 = mod.reshape(depth * batch, 1, 6 * d)
    bias = _bias_tiles(rel_bias)
    xf = x.reshape(batch * seq, d)

    for i in range(depth):
        jdx = i // 2
        mod_row = i * batch
        gain_mix = norm_mix[i].reshape(1, d)
        if i % 2 == 0:
            scale = HEAD_DIM ** -0.5 * LOG2E
            ones = jnp.ones((ATT_HEADS * HEAD_DIM,), F32)
            head_gain = jnp.concatenate(
                [jnp.concatenate([jnp.tile(att_q_gain[jdx, g] * scale, ATT_HEADS),
                                  jnp.tile(att_k_gain[jdx, g], ATT_HEADS), ones])
                 for g in range(N_GROUPS)]).reshape(1, -1)
            qkv = _att_proj(xf, gain_mix, mod3, mod_row, att_w_qkv[jdx].astype(BF16), head_gain,
                            tiles_per_batch)
            outs, lses = zip(*[_attention_group(qkv, bias, g, batch, seq)
                               for g in range(N_GROUPS)])
            att_tm = 512
            xf = _att_out_proj(outs, lses, att_w_o[jdx].astype(BF16), xf, mod3, mod_row,
                               seq // att_tm, att_tm)
        else:
            qkvg = _proj(xf, gain_mix, mod3, mod_row, ret_w_qkvg[jdx].astype(BF16), seq,
                         2 * RET_HEADS * RET_HEAD_QK + RET_HEADS * RET_HEAD_V)
            mixed = _retention(qkvg, batch, seq)
            xf = _out_proj(mixed, ret_w_o[jdx].astype(BF16), xf, mod3, mod_row, 2,
                           tiles_per_batch, TM, TN, "ret_out_proj")

        act = _ffn_up(xf, norm_ffn[i].reshape(1, d), mod3, mod_row, ffn_w_up[i].astype(BF16),
                      seq)
        xf = _out_proj(act, ffn_w_down[i].astype(BF16), xf, mod3, mod_row, 5, tiles_per_batch,
                       TM, 512, "ffn_down")
    return xf.reshape(batch, seq, d)
```

```python
import functools
import math

import numpy as np
import jax
import jax.numpy as jnp
from jax import lax
from jax.experimental import pallas as pl
from jax.experimental.pallas import tpu as pltpu

F32 = jnp.float32
BF16 = jnp.bfloat16

NORM_EPS = 1e-6
MASK_VALUE = -1e30
LOG2E = math.log2(math.e)
LANES = 128

ATT_HEADS = 16
HEAD_DIM = 128
DILATED_PATTERNS = ((128, 1), (512, 4), (2048, 16))
N_GROUPS = len(DILATED_PATTERNS)
ATT_BLK = 128
REL_BUCKETS = 32
REL_MAX_DIST = 2048

RET_HEADS = 8
RET_HEAD_QK = 256
RET_HEAD_V = 512
RET_CHUNK = 128

VMEM_LIMIT_BYTES = 56 * 1024 * 1024

TM = 1024
TN = 1024
TN_WIDE = 2048
ROW_CHUNK = 256
MM_ROWS = 256
STAGE_COLS = 4


def _params(n_axes):
    return pltpu.CompilerParams(dimension_semantics=("arbitrary",) * n_axes,
                                vmem_limit_bytes=VMEM_LIMIT_BYTES)


def _silu(v):
    return v / (1.0 + jnp.exp(-v))


def _row_chunk(i, size):
    if isinstance(i, int):
        return pl.ds(i * size, size)
    return pl.ds(pl.multiple_of(i * size, size), size)


def _mod_kernel(c_ref, w_ref, b_ref, o_ref):
    cond = _silu(c_ref[...]).astype(BF16)
    o_ref[...] = (jnp.dot(cond, w_ref[...].astype(BF16), preferred_element_type=F32)
                  + b_ref[...])


def _modulation(c, w_mod, b_mod):
    depth, d, n = w_mod.shape
    b = c.shape[0]
    tn = 1024
    return pl.pallas_call(
        _mod_kernel,
        grid=(depth, n // tn),
        in_specs=[pl.BlockSpec((b, d), lambda l, j: (0, 0)),
                  pl.BlockSpec((None, d, tn), lambda l, j: (l, 0, j)),
                  pl.BlockSpec((None, 1, tn), lambda l, j: (l, 0, j))],
        out_specs=pl.BlockSpec((None, b, tn), lambda l, j: (l, 0, j)),
        out_shape=jax.ShapeDtypeStruct((depth, b, n), F32),
        compiler_params=_params(2),
        name="adaln_mod",
    )(c, w_mod, b_mod.reshape(depth, 1, n))


def _row_rsqrt_to_scratch(x_ref, rs_ref):
    def body(i, carry):
        rows = _row_chunk(i, ROW_CHUNK)
        xx = x_ref[rows, :]
        ms = jnp.mean(xx * xx, axis=-1, keepdims=True)
        rs_ref[rows, :] = jnp.broadcast_to(lax.rsqrt(ms + NORM_EPS), (ROW_CHUNK, LANES))
        return carry

    lax.fori_loop(0, x_ref.shape[0] // ROW_CHUNK, body, 0)


def _norm_mod_to_scratch(x_ref, gain_ref, sc_ref, sh_ref, h_ref, rs_ref):
    _row_rsqrt_to_scratch(x_ref, rs_ref)

    def body(i, carry):
        rows = _row_chunk(i, ROW_CHUNK)
        rs = rs_ref[rows, :]
        for cb in range(x_ref.shape[1] // LANES):
            cols = slice(cb * LANES, (cb + 1) * LANES)
            hn = ((x_ref[rows, cols] * rs * gain_ref[:, cols]) * (1.0 + sc_ref[:, cols])
                  + sh_ref[:, cols])
            h_ref[rows, cols] = hn.astype(BF16)
        return carry

    lax.fori_loop(0, x_ref.shape[0] // ROW_CHUNK, body, 0)


def _x_tile_spec(tm, d):
    return pl.BlockSpec((tm, d), lambda i, j: (i, 0), pipeline_mode=pl.Buffered(1))


def _mod_spec(d, tiles_per_batch, row, chunk):
    return pl.BlockSpec((None, 1, d), lambda i, j: (row + i // tiles_per_batch, 0, chunk))


def _att_prologue(x_ref, gain_ref, sc_ref, sh_ref, h_ref, rs_ref, stage_ref, stage2_ref):
    tm, d = x_ref.shape
    n_chunks = tm // ROW_CHUNK

    _row_rsqrt_to_scratch(x_ref, rs_ref)

    for cg in range(d // (STAGE_COLS * LANES)):
        col = lambda cc: slice((cg * STAGE_COLS + cc) * LANES, (cg * STAGE_COLS + cc + 1) * LANES)

        def natural(i, carry):
            rows = _row_chunk(i, ROW_CHUNK)
            rs = rs_ref[rows, :]
            for cc in range(STAGE_COLS):
                cols = col(cc)
                hn = ((x_ref[rows, cols] * rs * gain_ref[:, cols]) * (1.0 + sc_ref[:, cols])
                      + sh_ref[:, cols])
                h_ref[0, rows, cols] = hn.astype(BF16)
                stage_ref[cc, rows, :] = hn
            return carry

        lax.fori_loop(0, n_chunks, natural, 0)

        dil = DILATED_PATTERNS[1][1]
        per = tm // dil
        sub = DILATED_PATTERNS[2][1] // dil
        per16 = per // sub

        def permute4(r, carry):
            dst = _row_chunk(r, per)
            for cc in range(STAGE_COLS):
                v = stage_ref[cc, pl.ds(r, per, stride=dil), :]
                h_ref[1, dst, col(cc)] = v.astype(BF16)
                stage2_ref[cc, dst, :] = v
            return carry

        lax.fori_loop(0, dil, permute4, 0)

        def permute16(r, carry):
            src = pl.ds((r % dil) * per + r // dil, per16, stride=sub)
            dst = _row_chunk(r, per16)
            for cc in range(STAGE_COLS):
                h_ref[2, dst, col(cc)] = stage2_ref[cc, src, :].astype(BF16)
            return carry

        lax.fori_loop(0, dil * sub, permute16, 0)


def _att_proj_kernel(x_ref, gain_ref, sc_ref, sh_ref, w_ref, hg_ref, o_ref, h_ref, rs_ref,
                     stage_ref, stage2_ref, *, d_model):
    j = pl.program_id(1)

    @pl.when(j == 0)
    def _():
        _att_prologue(x_ref, gain_ref, sc_ref, sh_ref, h_ref, rs_ref, stage_ref, stage2_ref)

    tn = w_ref.shape[1]
    group = (j * tn) // (3 * d_model)
    is_v = ((j * tn) // d_model) % 3 == 2

    def tile(head_norm):
        for rc in range(h_ref.shape[1] // MM_ROWS):
            rows = slice(rc * MM_ROWS, (rc + 1) * MM_ROWS)
            acc = jnp.dot(h_ref[group, rows, :], w_ref[...], preferred_element_type=F32)
            if not head_norm:
                o_ref[rows, :] = acc.astype(BF16)
                continue
            for hh in range(tn // HEAD_DIM):
                sl = slice(hh * HEAD_DIM, (hh + 1) * HEAD_DIM)
                a = acc[:, sl]
                ms = jnp.mean(a * a, axis=-1, keepdims=True)
                o_ref[rows, sl] = (a * lax.rsqrt(ms + NORM_EPS) * hg_ref[:, sl]).astype(BF16)

    pl.when(is_v)(functools.partial(tile, False))
    pl.when(jnp.logical_not(is_v))(functools.partial(tile, True))


def _att_proj(x, gain, mod3, mod_row, w, head_gain, tiles_per_batch):
    t, d = x.shape
    n = w.shape[1]
    return pl.pallas_call(
        functools.partial(_att_proj_kernel, d_model=d),
        grid=(t // TM, n // TN_WIDE),
        in_specs=[_x_tile_spec(TM, d),
                  pl.BlockSpec((1, d), lambda i, j: (0, 0)),
                  _mod_spec(d, tiles_per_batch, mod_row, 1),
                  _mod_spec(d, tiles_per_batch, mod_row, 0),
                  pl.BlockSpec((d, TN_WIDE), lambda i, j: (0, j)),
                  pl.BlockSpec((1, TN_WIDE), lambda i, j: (0, j))],
        out_specs=pl.BlockSpec((TM, TN_WIDE), lambda i, j: (i, j)),
        out_shape=jax.ShapeDtypeStruct((t, n), BF16),
        scratch_shapes=[pltpu.VMEM((N_GROUPS, TM, d), BF16),
                        pltpu.VMEM((TM, LANES), F32),
                        pltpu.VMEM((STAGE_COLS, TM, LANES), F32),
                        pltpu.VMEM((STAGE_COLS, TM, LANES), F32)],
        compiler_params=_params(2),
        name="att_qkv_proj",
    )(x, gain, mod3, mod3, w, head_gain)


def _proj_kernel(x_ref, gain_ref, sc_ref, sh_ref, w_ref, o_ref, h_ref, rs_ref, *, gate_col0):
    j = pl.program_id(1)

    @pl.when(j == 0)
    def _():
        _norm_mod_to_scratch(x_ref, gain_ref, sc_ref, sh_ref, h_ref, rs_ref)

    is_gate = j * w_ref.shape[1] >= gate_col0

    def tile(gate):
        for rc in range(h_ref.shape[0] // MM_ROWS):
            rows = slice(rc * MM_ROWS, (rc + 1) * MM_ROWS)
            acc = jnp.dot(h_ref[rows, :], w_ref[...], preferred_element_type=F32)
            o_ref[rows, :] = (_silu(acc) if gate else acc).astype(BF16)

    pl.when(is_gate)(functools.partial(tile, True))
    pl.when(jnp.logical_not(is_gate))(functools.partial(tile, False))


def _proj(x, gain, mod3, mod_row, w, seq, gate_col0):
    t, d = x.shape
    n = w.shape[1]
    tm = TM
    tiles_per_batch = seq // tm
    return pl.pallas_call(
        functools.partial(_proj_kernel, gate_col0=gate_col0),
        grid=(t // tm, n // TN),
        in_specs=[pl.BlockSpec((tm, d), lambda i, j: (i, 0)),
                  pl.BlockSpec((1, d), lambda i, j: (0, 0)),
                  _mod_spec(d, tiles_per_batch, mod_row, 1),
                  _mod_spec(d, tiles_per_batch, mod_row, 0),
                  pl.BlockSpec((d, TN), lambda i, j: (0, j))],
        out_specs=pl.BlockSpec((tm, TN), lambda i, j: (i, j)),
        out_shape=jax.ShapeDtypeStruct((t, n), BF16),
        scratch_shapes=[pltpu.VMEM((tm, d), BF16), pltpu.VMEM((tm, LANES), F32)],
        compiler_params=_params(2),
        name="ret_qkvg_proj",
    )(x, gain, mod3, mod3, w)


def _ffn_up_kernel(x_ref, gain_ref, sc_ref, sh_ref, wg_ref, wu_ref, o_ref, h_ref, rs_ref):
    @pl.when(pl.program_id(1) == 0)
    def _():
        _norm_mod_to_scratch(x_ref, gain_ref, sc_ref, sh_ref, h_ref, rs_ref)

    for rc in range(h_ref.shape[0] // MM_ROWS):
        rows = slice(rc * MM_ROWS, (rc + 1) * MM_ROWS)
        h = h_ref[rows, :]
        gate = jnp.dot(h, wg_ref[...], preferred_element_type=F32)
        up = jnp.dot(h, wu_ref[...], preferred_element_type=F32)
        o_ref[rows, :] = (_silu(gate) * up).astype(BF16)


def _ffn_up(x, gain, mod3, mod_row, w_up, seq):
    t, d = x.shape
    d_ff = w_up.shape[1] // 2
    tf = 512
    n_f = d_ff // tf
    tm = TM
    tiles_per_batch = seq // tm
    return pl.pallas_call(
        _ffn_up_kernel,
        grid=(t // tm, n_f),
        in_specs=[pl.BlockSpec((tm, d), lambda i, j: (i, 0)),
                  pl.BlockSpec((1, d), lambda i, j: (0, 0)),
                  _mod_spec(d, tiles_per_batch, mod_row, 4),
                  _mod_spec(d, tiles_per_batch, mod_row, 3),
                  pl.BlockSpec((d, tf), lambda i, j: (0, j)),
                  pl.BlockSpec((d, tf), lambda i, j: (0, j + n_f))],
        out_specs=pl.BlockSpec((tm, tf), lambda i, j: (i, j)),
        out_shape=jax.ShapeDtypeStruct((t, d_ff), BF16),
        scratch_shapes=[pltpu.VMEM((tm, d), BF16), pltpu.VMEM((tm, LANES), F32)],
        compiler_params=_params(2),
        name="ffn_up",
    )(x, gain, mod3, mod3, w_up, w_up)


def _out_proj_kernel(a_ref, w_ref, x_ref, g_ref, o_ref):
    for rc in range(a_ref.shape[0] // MM_ROWS):
        rows = slice(rc * MM_ROWS, (rc + 1) * MM_ROWS)
        y = jnp.dot(a_ref[rows, :], w_ref[...], preferred_element_type=F32)
        o_ref[rows, :] = x_ref[rows, :] + g_ref[...] * y


def _gate_spec(n, tn, mod_row, gate_chunk, tiles_per_batch):
    return pl.BlockSpec((None, 1, tn),
                        lambda i, j: (mod_row + i // tiles_per_batch, 0,
                                      gate_chunk * (n // tn) + j))


def _out_proj(a, w, x, mod3, mod_row, gate_chunk, tiles_per_batch, tm, tn, name):
    t, k = a.shape
    n = w.shape[1]
    return pl.pallas_call(
        _out_proj_kernel,
        grid=(t // tm, n // tn),
        in_specs=[pl.BlockSpec((tm, k), lambda i, j: (i, 0)),
                  pl.BlockSpec((k, tn), lambda i, j: (0, j)),
                  pl.BlockSpec((tm, tn), lambda i, j: (i, j)),
                  _gate_spec(n, tn, mod_row, gate_chunk, tiles_per_batch)],
        out_specs=pl.BlockSpec((tm, tn), lambda i, j: (i, j)),
        out_shape=jax.ShapeDtypeStruct((t, n), F32),
        compiler_params=_params(2),
        name=name,
    )(a, w, x, mod3)


def _bias_kernel(tab_ref, bucket_ref, valid_ref, o_ref):
    g = pl.program_id(0)
    h = pl.program_id(1)
    bucket = bucket_ref[...]
    col = g * ATT_HEADS + h
    acc = jnp.zeros(bucket.shape, F32)
    for b in range(REL_BUCKETS):
        acc = jnp.where(bucket == b, tab_ref[b, col] * LOG2E, acc)
    for first in range(2):
        o_ref[first] = jnp.where(valid_ref[first] != 0, acc, MASK_VALUE)


def _t5_bucket_np(dist):
    max_exact = REL_BUCKETS // 2
    d_f = np.maximum(dist, 1).astype(np.float32)
    large = max_exact + (np.log(d_f / np.float32(max_exact))
                         / np.float32(math.log(REL_MAX_DIST / max_exact))
                         * np.float32(REL_BUCKETS - max_exact)).astype(np.int32)
    large = np.minimum(large, REL_BUCKETS - 1)
    return np.where(dist < max_exact, dist, large).astype(np.int32)


def _bias_tiles(rel_bias):
    blk = ATT_BLK
    qi = np.arange(blk)[:, None]
    ki = np.arange(2 * blk)[None, :]
    delta = blk + qi - ki
    band = (delta >= 0) & (delta <= blk)
    valid = np.stack([band & (ki >= blk), band]).astype(np.int32)
    buckets = np.stack([_t5_bucket_np(np.maximum(delta, 0) * dil)
                        for _, dil in DILATED_PATTERNS])
    return pl.pallas_call(
        _bias_kernel,
        grid=(N_GROUPS, ATT_HEADS),
        in_specs=[pl.BlockSpec(memory_space=pltpu.SMEM),
                  pl.BlockSpec((None, blk, 2 * blk), lambda g, h: (g, 0, 0)),
                  pl.BlockSpec((2, blk, 2 * blk), lambda g, h: (0, 0, 0))],
        out_specs=pl.BlockSpec((None, 2, None, blk, 2 * blk), lambda g, h: (g, 0, h, 0, 0)),
        out_shape=jax.ShapeDtypeStruct((N_GROUPS, 2, ATT_HEADS, blk, 2 * blk), F32),
        compiler_params=_params(2),
        name="rel_bias_tiles",
    )(rel_bias, jnp.asarray(buckets), jnp.asarray(valid))


ATT_PASS = 8


def _att_units(q_ref, kc_ref, kp_ref, vc_ref, vp_ref, bias_ref, store, *, g, heads, first_span):
    blk = ATT_BLK
    dil = DILATED_PATTERNS[g][1]
    dn = (((1,), (1,)), ((), ()))
    ones = jnp.ones((2 * blk, HEAD_DIM), BF16)
    lane = lax.broadcasted_iota(jnp.int32, (blk, LANES), 1)
    head0 = pl.program_id(1) * heads

    def group(units):
        sl = lambda h: slice(h * HEAD_DIM, (h + 1) * HEAD_DIM)
        pairs = [(ui, h) for ui in range(len(units)) for h in range(heads)]
        m_tiles = [jnp.zeros((blk, LANES), F32) for _ in units]
        l_tiles = [jnp.ones((blk, LANES), F32) for _ in units]
        for p0 in range(0, len(pairs), ATT_PASS):
            batch = pairs[p0:p0 + ATT_PASS]
            pieces = []
            for ui, h in batch:
                _, q_of, k_of, _, variant = units[ui]
                pieces.append(lax.dot_general(q_of(sl(h)), k_of(sl(h)), dn,
                                              preferred_element_type=F32)
                              + bias_ref[variant, h])
            s = jnp.concatenate(pieces, axis=0)
            m = jnp.max(jnp.maximum(s[:, :blk], s[:, blk:]), axis=-1, keepdims=True)
            p = jnp.exp2(s - m).astype(BF16)
            for idx, (ui, h) in enumerate(batch):
                u, _, _, v_of, _ = units[ui]
                rows = slice(idx * blk, (idx + 1) * blk)
                v_ext = jnp.concatenate([v_of(sl(h)), ones], axis=1)
                pv = jnp.dot(p[rows], v_ext, preferred_element_type=F32)
                denom = pv[:, HEAD_DIM:]
                o = pv[:, :HEAD_DIM] / denom
                mine = lane == head0 + h
                m_tiles[ui] = jnp.where(mine, m[rows], m_tiles[ui])
                l_tiles[ui] = jnp.where(mine, denom, l_tiles[ui])
                store(u, h, o, None)
        for ui, unit_args in enumerate(units):
            store(unit_args[0], None, None, m_tiles[ui] + jnp.log2(l_tiles[ui]))

    unit = lambda *unit_args: group([unit_args])

    if dil == 1:
        n_units = q_ref.shape[0] // blk
        unit(0,
             lambda sl: q_ref[0:blk, sl],
             lambda sl: jnp.concatenate([kp_ref[:, sl], kc_ref[0:blk, sl]], axis=0),
             lambda sl: jnp.concatenate([vp_ref[:, sl], vc_ref[0:blk, sl]], axis=0),
             jnp.where(first_span, 0, 1))

        def body(u, carry):
            cur = _row_chunk(u, blk)
            both = pl.ds(pl.multiple_of((u - 1) * blk, blk), 2 * blk)
            unit(u, lambda sl: q_ref[cur, sl], lambda sl: kc_ref[both, sl],
                 lambda sl: vc_ref[both, sl], 1)
            return carry

        lax.fori_loop(1, n_units, body, 0)
    else:
        variant = jnp.where(first_span, 0, 1)
        if dil == 4:
            pick = lambda ref, r, sl: ref[r, :, sl]
            prev_cur = lambda pref, cref, r, sl: jnp.concatenate(
                [pref[r, :, sl], cref[r, :, sl]], axis=0)
        else:
            pick = lambda ref, r, sl: jnp.concatenate([ref[0, r, :, sl], ref[1, r, :, sl]], axis=0)
            prev_cur = lambda pref, cref, r, sl: jnp.concatenate(
                [pref[0, r, :, sl], pref[1, r, :, sl], cref[0, r, :, sl], cref[1, r, :, sl]],
                axis=0)

        per_iter = ATT_HEADS // heads

        def body(it, carry):
            def unit_args(r):
                return (r, lambda sl: pick(q_ref, r, sl),
                        lambda sl: prev_cur(kp_ref, kc_ref, r, sl),
                        lambda sl: prev_cur(vp_ref, vc_ref, r, sl), variant)

            group([unit_args(it * per_iter + sub) for sub in range(per_iter)])
            return carry

        lax.fori_loop(0, dil // per_iter, body, 0)


def _att_kernel(q_ref, kc_ref, vc_ref, bias_ref, o_ref, lse_ref, kp_ref, vp_ref, *scratch,
                g, heads):
    blk = ATT_BLK
    dil = DILATED_PATTERNS[g][1]
    first_span = pl.program_id(2) == 0

    @pl.when(first_span)
    def _():
        kp_ref[...] = jnp.zeros_like(kp_ref)
        vp_ref[...] = jnp.zeros_like(vp_ref)

    if dil == 1:
        def store(u, h, o, lse_tile):
            rows = _row_chunk(u, blk)
            if h is None:
                lse_ref[rows, :] = lse_tile
            else:
                o_ref[rows, h * HEAD_DIM:(h + 1) * HEAD_DIM] = o.astype(BF16)
    else:
        o_scr, lse_scr = scratch

        def store(u, h, o, lse_tile):
            rows = pl.ds(u, blk, stride=dil)
            if h is None:
                lse_scr[rows, :] = lse_tile
            else:
                o_scr[h, rows, :] = o

    _att_units(q_ref, kc_ref, kp_ref, vc_ref, vp_ref, bias_ref, store, g=g, heads=heads,
               first_span=first_span)

    if dil == 1:
        last = slice(kc_ref.shape[0] - blk, kc_ref.shape[0])
        kp_ref[...] = kc_ref[last, :]
        vp_ref[...] = vc_ref[last, :]
    else:
        kp_ref[...] = kc_ref[...]
        vp_ref[...] = vc_ref[...]
        for h in range(heads):
            o_ref[:, h * HEAD_DIM:(h + 1) * HEAD_DIM] = o_scr[h].astype(BF16)
        lse_ref[...] = lse_scr[...]


def _attention_group(qkv, bias, g, batch, seq):
    dil = DILATED_PATTERNS[g][1]
    blk = ATT_BLK
    span = dil * blk if dil > 1 else 4 * blk
    heads = ATT_HEADS if dil < 16 else 4
    n_hg = ATT_HEADS // heads
    width = heads * HEAD_DIM
    n_cols = qkv.shape[1]
    col0 = lambda which: (g * 3 + which) * (ATT_HEADS // heads)

    if dil == 1:
        view = qkv.reshape(batch, seq, n_cols)
        cur = lambda which: pl.BlockSpec((None, span, width),
                                         lambda b, h, s: (b, s, col0(which) + h))
        prev_shape = (blk, width)
        scratch = []
    elif dil == 4:
        halves = TM // (dil * blk)
        view = qkv.reshape(batch, seq // TM, dil, halves, blk, n_cols)
        cur = lambda which: pl.BlockSpec(
            (None, None, dil, None, blk, width),
            lambda b, h, s: (b, s // halves, 0, s % halves, 0, col0(which) + h))
        prev_shape = (dil, blk, width)
        scratch = [pltpu.VMEM((heads, span, HEAD_DIM), F32), pltpu.VMEM((span, LANES), F32)]
    else:
        tiles = span // TM
        per = TM // dil
        view = qkv.reshape(batch, seq // span, tiles, dil, per, n_cols)
        cur = lambda which: pl.BlockSpec((None, None, tiles, dil, per, width),
                                         lambda b, h, s: (b, s, 0, 0, 0, col0(which) + h))
        prev_shape = (tiles, dil, per, width)
        scratch = [pltpu.VMEM((heads, span, HEAD_DIM), F32), pltpu.VMEM((span, LANES), F32)]

    o, lse = pl.pallas_call(
        functools.partial(_att_kernel, g=g, heads=heads),
        grid=(batch, n_hg, seq // span),
        in_specs=[cur(0), cur(1), cur(2),
                  pl.BlockSpec((None, 2, heads, blk, 2 * blk), lambda b, h, s: (g, 0, h, 0, 0))],
        out_specs=[pl.BlockSpec((None, span, width), lambda b, h, s: (b, s, h)),
                   pl.BlockSpec((None, None, span, LANES), lambda b, h, s: (b, h, s, 0))],
        out_shape=[jax.ShapeDtypeStruct((batch, seq, ATT_HEADS * HEAD_DIM), BF16),
                   jax.ShapeDtypeStruct((batch, n_hg, seq, LANES), F32)],
        scratch_shapes=[pltpu.VMEM(prev_shape, BF16), pltpu.VMEM(prev_shape, BF16)] + scratch,
        compiler_params=_params(3),
        name=f"dilated_attention_g{g}",
    )(view, view, view, bias)
    return o, lse


MERGE_ROWS = 256


def _att_out_kernel(o0_ref, o1_ref, o2_ref, l0_ref, l1_ref, l2_ref, w_ref, x_ref, g_ref,
                    out_ref):
    o_refs = (o0_ref, o1_ref, o2_ref)
    l_refs = (l0_ref, l1_ref, l2_ref)
    tm = x_ref.shape[0]
    lane = lax.broadcasted_iota(jnp.int32, (MERGE_ROWS, LANES), 1)
    for rc in range(tm // MERGE_ROWS):
        rows = slice(rc * MERGE_ROWS, (rc + 1) * MERGE_ROWS)
        lses = []
        for l_ref in l_refs:
            tile = l_ref[0, rows, :]
            for hg in range(1, l_ref.shape[0]):
                tile = tile + l_ref[hg, rows, :]
            lses.append(tile)
        mx = jnp.maximum(jnp.maximum(lses[0], lses[1]), lses[2])
        es = [jnp.exp2(t - mx) for t in lses]
        den = es[0] + es[1] + es[2]
        ws = [es[0] / den, es[1] / den]
        pieces = []
        for h in range(ATT_HEADS):
            sl = slice(h * HEAD_DIM, (h + 1) * HEAD_DIM)
            base = o_refs[2][rows, sl].astype(F32)
            acc = base
            for w_g, o_ref in zip(ws, o_refs[:2]):
                w_col = jnp.sum(jnp.where(lane == h, w_g, 0.0), axis=-1, keepdims=True)
                acc = acc + w_col * (o_ref[rows, sl].astype(F32) - base)
            pieces.append(acc.astype(BF16))
        merged = jnp.concatenate(pieces, axis=1)
        y = jnp.dot(merged, w_ref[...], preferred_element_type=F32)
        out_ref[rows, :] = x_ref[rows, :] + g_ref[...] * y


def _att_out_proj(outs, lses, w, x, mod3, mod_row, tiles_per_batch_tm, tm):
    t, n = x.shape
    k = w.shape[0]
    seq_tiles = tiles_per_batch_tm
    o_spec = pl.BlockSpec((tm, k), lambda i, j: (i, 0))
    l_spec = lambda n_hg: pl.BlockSpec((None, n_hg, tm, LANES),
                                       lambda i, j: (i // seq_tiles, 0, i % seq_tiles, 0))
    return pl.pallas_call(
        _att_out_kernel,
        grid=(t // tm, 1),
        in_specs=[o_spec, o_spec, o_spec] + [l_spec(l.shape[1]) for l in lses]
                 + [pl.BlockSpec((k, n), lambda i, j: (0, 0)),
                    pl.BlockSpec((tm, n), lambda i, j: (i, 0)),
                    _gate_spec(n, n, mod_row, 2, seq_tiles)],
        out_specs=pl.BlockSpec((tm, n), lambda i, j: (i, 0)),
        out_shape=jax.ShapeDtypeStruct((t, n), F32),
        compiler_params=_params(2),
        name="att_merge_out_proj",
    )(*[o.reshape(t, k) for o in outs], *lses, w, x, mod3)


RET_TOKENS = 1024
RET_HEADS_PER_STEP = 2


def _retention_kernel(q_ref, k_ref, v_ref, g_ref, cos_ref, sin_ref, inner_ref, cross_ref,
                      sdec_ref, cdec_ref, o_ref, state_ref):
    @pl.when(pl.program_id(2) == 0)
    def _():
        state_ref[...] = jnp.zeros_like(state_ref)

    half = RET_HEAD_QK // 2
    dn_nt = (((1,), (1,)), ((), ()))
    dn_tn = (((0,), (0,)), ((), ()))
    for c in range(RET_TOKENS // RET_CHUNK):
        rows = slice(c * RET_CHUNK, (c + 1) * RET_CHUNK)
        cos = cos_ref[rows, :]
        sin = sin_ref[rows, :]
        for hh in range(RET_HEADS_PER_STEP):
            qk_cols = lambda part: slice(hh * RET_HEAD_QK + part * half,
                                         hh * RET_HEAD_QK + (part + 1) * half)
            v_cols = slice(hh * RET_HEAD_V, (hh + 1) * RET_HEAD_V)
            q1 = q_ref[rows, qk_cols(0)].astype(F32)
            q2 = q_ref[rows, qk_cols(1)].astype(F32)
            k1 = k_ref[rows, qk_cols(0)].astype(F32)
            k2 = k_ref[rows, qk_cols(1)].astype(F32)
            cross = cross_ref[hh]
            sdec = sdec_ref[hh]
            qr1 = q1 * cos - q2 * sin
            qr2 = q1 * sin + q2 * cos
            kr1 = k1 * cos - k2 * sin
            kr2 = k1 * sin + k2 * cos
            q_b = jnp.concatenate([qr1, qr2], axis=1).astype(BF16)
            qc_b = jnp.concatenate([qr1 * cross, qr2 * cross], axis=1).astype(BF16)
            k_b = jnp.concatenate([kr1, kr2], axis=1).astype(BF16)
            ks_b = jnp.concatenate([kr1 * sdec, kr2 * sdec], axis=1).astype(BF16)
            v = v_ref[rows, v_cols]

            scores = (lax.dot_general(q_b, k_b, dn_nt, preferred_element_type=F32)
                      * inner_ref[hh])
            state = state_ref[hh]
            o = (jnp.dot(scores.astype(BF16), v, preferred_element_type=F32)
                 + jnp.dot(qc_b, state.astype(BF16), preferred_element_type=F32))
            state_ref[hh] = state * cdec_ref[hh] + lax.dot_general(
                ks_b, v, dn_tn, preferred_element_type=F32)

            ms = jnp.mean(o * o, axis=-1, keepdims=True)
            gate = g_ref[rows, v_cols].astype(F32)
            o_ref[rows, v_cols] = (gate * (o * lax.rsqrt(ms + NORM_EPS))).astype(BF16)


def _retention(qkvg, batch, seq):
    c = RET_CHUNK
    hq, hv = RET_HEAD_QK, RET_HEAD_V
    hps = RET_HEADS_PER_STEP
    n_qk = RET_HEADS * hq
    n_v = RET_HEADS * hv

    half = hq // 2
    inv = 1.0 / (10000.0 ** jnp.linspace(0.0, 1.0, half, dtype=F32))
    ang = jnp.arange(seq, dtype=F32)[:, None] * inv[None, :]
    cos, sin = jnp.cos(ang), jnp.sin(ang)
    log_gamma = jnp.log(1.0 - 2.0 ** (-5.0 - jnp.arange(RET_HEADS, dtype=F32)))
    pos = jnp.arange(c, dtype=F32)
    diff = pos[:, None] - pos[None, :]
    inner = jnp.where(diff[None] >= 0,
                      jnp.exp(jnp.maximum(diff, 0.0)[None] * log_gamma[:, None, None]), 0.0)
    cross = jnp.exp((pos[None, :] + 1.0) * log_gamma[:, None])
    sdec = jnp.exp((c - 1.0 - pos)[None, :] * log_gamma[:, None])
    cdec = jnp.exp(c * log_gamma)
    k_scale = hq ** -0.5
    inner = inner * k_scale
    cross_l = jnp.broadcast_to(cross[:, :, None], (RET_HEADS, c, half))
    sdec_l = jnp.broadcast_to((sdec * k_scale)[:, :, None], (RET_HEADS, c, half))
    cdec_l = jnp.broadcast_to(cdec[:, None, None], (RET_HEADS, 1, hv))

    x = qkvg.reshape(batch, seq, qkvg.shape[1])
    tt = RET_TOKENS
    wq, wv = hps * hq, hps * hv
    head_tab = lambda shape: pl.BlockSpec((hps,) + shape, lambda b, h, s: (h, 0, 0))
    out = pl.pallas_call(
        _retention_kernel,
        grid=(batch, RET_HEADS // hps, seq // tt),
        in_specs=[pl.BlockSpec((None, tt, wq), lambda b, h, s: (b, s, h)),
                  pl.BlockSpec((None, tt, wq), lambda b, h, s: (b, s, n_qk // wq + h)),
                  pl.BlockSpec((None, tt, wv), lambda b, h, s: (b, s, 2 * n_qk // wv + h)),
                  pl.BlockSpec((None, tt, wv), lambda b, h, s: (b, s, (2 * n_qk + n_v) // wv + h)),
                  pl.BlockSpec((tt, half), lambda b, h, s: (s, 0)),
                  pl.BlockSpec((tt, half), lambda b, h, s: (s, 0)),
                  head_tab((c, c)), head_tab((c, half)), head_tab((c, half)), head_tab((1, hv))],
        out_specs=pl.BlockSpec((None, tt, wv), lambda b, h, s: (b, s, h)),
        out_shape=jax.ShapeDtypeStruct((batch, seq, n_v), BF16),
        scratch_shapes=[pltpu.VMEM((hps, hq, hv), F32)],
        compiler_params=_params(3),
        name="retention",
    )(x, x, x, x, cos, sin, inner, cross_l, sdec_l, cdec_l)
    return out.reshape(batch * seq, n_v)


def kernel(x, c, w_mod, b_mod, norm_mix, norm_ffn, rel_bias, att_w_qkv, att_q_gain, att_k_gain,
           att_w_o, ret_w_qkvg, ret_w_o, ffn_w_up, ffn_w_down):
    batch, seq, d = x.shape
    depth = w_mod.shape[0]
    assert seq % (2 * TM) == 0 and d == ATT_HEADS * HEAD_DIM
    tiles_per_batch = seq // TM

    mod = _modulation(c, w_mod, b_mod)
    mod3 = mod.reshape(depth * batch, 1, 6 * d)
    bias = _bias_tiles(rel_bias)
    xf = x.reshape(batch * seq, d)

    for i in range(depth):
        jdx = i // 2
        mod_row = i * batch
        gain_mix = norm_mix[i].reshape(1, d)
        if i % 2 == 0:
            scale = HEAD_DIM ** -0.5 * LOG2E
            ones = jnp.ones((ATT_HEADS * HEAD_DIM,), F32)
            head_gain = jnp.concatenate(
                [jnp.concatenate([jnp.tile(att_q_gain[jdx, g] * scale, ATT_HEADS),
                                  jnp.tile(att_k_gain[jdx, g], ATT_HEADS), ones])
                 for g in range(N_GROUPS)]).reshape(1, -1)
            qkv = _att_proj(xf, gain_mix, mod3, mod_row, att_w_qkv[jdx].astype(BF16), head_gain,
                            tiles_per_batch)
            outs, lses = zip(*[_attention_group(qkv, bias, g, batch, seq)
                               for g in range(N_GROUPS)])
            att_tm = 512
            xf = _att_out_proj(outs, lses, att_w_o[jdx].astype(BF16), xf, mod3, mod_row,
                               seq // att_tm, att_tm)
        else:
            qkvg = _proj(xf, gain_mix, mod3, mod_row, ret_w_qkvg[jdx].astype(BF16), seq,
                         2 * RET_HEADS * RET_HEAD_QK + RET_HEADS * RET_HEAD_V)
            mixed = _retention(qkvg, batch, seq)
            xf = _out_proj(mixed, ret_w_o[jdx].astype(BF16), xf, mod3, mod_row, 2,
                           tiles_per_batch, TM, TN, "ret_out_proj")

        act = _ffn_up(xf, norm_ffn[i].reshape(1, d), mod3, mod_row, ffn_w_up[i].astype(BF16),
                      seq)
        xf = _out_proj(act, ffn_w_down[i].astype(BF16), xf, mod3, mod_row, 5, tiles_per_batch,
                       TM, 512, "ffn_down")
    return xf.reshape(batch, seq, d)
```

```python
import functools
import math

import numpy as np
import jax
import jax.numpy as jnp
from jax import lax
from jax.experimental import pallas as pl
from jax.experimental.pallas import tpu as pltpu

F32 = jnp.float32
BF16 = jnp.bfloat16

NORM_EPS = 1e-6
MASK_VALUE = -1e30
LOG2E = math.log2(math.e)
LANES = 128

ATT_HEADS = 16
HEAD_DIM = 128
DILATED_PATTERNS = ((128, 1), (512, 4), (2048, 16))
N_GROUPS = len(DILATED_PATTERNS)
ATT_BLK = 128
REL_BUCKETS = 32
REL_MAX_DIST = 2048

RET_HEADS = 8
RET_HEAD_QK = 256
RET_HEAD_V = 512
RET_CHUNK = 256

VMEM_LIMIT_BYTES = 56 * 1024 * 1024

TM = 1024
TN = 1024
TN_WIDE = 2048
ROW_CHUNK = 256
MM_ROWS = 256
STAGE_COLS = 4
CAST_ROWS = 16


def _params(n_axes):
    return pltpu.CompilerParams(dimension_semantics=("arbitrary",) * n_axes,
                                vmem_limit_bytes=VMEM_LIMIT_BYTES)


def _silu(v):
    return v / (1.0 + jnp.exp(-v))


def _row_chunk(i, size):
    if isinstance(i, int):
        return pl.ds(i * size, size)
    return pl.ds(pl.multiple_of(i * size, size), size)


def _mod_kernel(c_ref, w_ref, b_ref, o_ref):
    cond = _silu(c_ref[...]).astype(BF16)
    o_ref[...] = (jnp.dot(cond, w_ref[...].astype(BF16), preferred_element_type=F32)
                  + b_ref[...])


def _modulation(c, w_mod, b_mod):
    depth, d, n = w_mod.shape
    b = c.shape[0]
    tn = 1024
    return pl.pallas_call(
        _mod_kernel,
        grid=(depth, n // tn),
        in_specs=[pl.BlockSpec((b, d), lambda l, j: (0, 0)),
                  pl.BlockSpec((None, d, tn), lambda l, j: (l, 0, j)),
                  pl.BlockSpec((None, 1, tn), lambda l, j: (l, 0, j))],
        out_specs=pl.BlockSpec((None, b, tn), lambda l, j: (l, 0, j)),
        out_shape=jax.ShapeDtypeStruct((depth, b, n), F32),
        compiler_params=_params(2),
        name="adaln_mod",
    )(c, w_mod, b_mod.reshape(depth, 1, n))


def _row_rsqrt_to_scratch(x_ref, rs_ref):
    def body(i, carry):
        rows = _row_chunk(i, ROW_CHUNK)
        xx = x_ref[rows, :]
        ms = jnp.mean(xx * xx, axis=-1, keepdims=True)
        rs_ref[rows, :] = jnp.broadcast_to(lax.rsqrt(ms + NORM_EPS), (ROW_CHUNK, LANES))
        return carry

    lax.fori_loop(0, x_ref.shape[0] // ROW_CHUNK, body, 0)


def _norm_mod_to_scratch(x_ref, gain_ref, sc_ref, sh_ref, h_ref, rs_ref):
    _row_rsqrt_to_scratch(x_ref, rs_ref)

    def body(i, carry):
        rows = _row_chunk(i, ROW_CHUNK)
        rs = rs_ref[rows, :]
        for cb in range(x_ref.shape[1] // LANES):
            cols = slice(cb * LANES, (cb + 1) * LANES)
            hn = ((x_ref[rows, cols] * rs * gain_ref[:, cols]) * (1.0 + sc_ref[:, cols])
                  + sh_ref[:, cols])
            h_ref[rows, cols] = hn.astype(BF16)
        return carry

    lax.fori_loop(0, x_ref.shape[0] // ROW_CHUNK, body, 0)


def _side_cast_specs(side, n_row_steps, n_cols_steps):
    specs, shapes = [], []
    for w in side:
        rows, cols = w.shape
        rb = CAST_ROWS
        while rows // rb > n_row_steps * n_cols_steps:
            rb *= 2
        assert rows % rb == 0
        last = rows // rb - 1
        specs.append(pl.BlockSpec(
            (rb, cols), lambda i, j, last=last: (jnp.minimum(i * n_cols_steps + j, last), 0)))
        shapes.append(jax.ShapeDtypeStruct(w.shape, BF16))
    return specs, shapes


def _side_cast(src_refs, dst_refs):
    for src, dst in zip(src_refs, dst_refs):
        dst[...] = src[...].astype(BF16)


def _x_tile_spec(tm, d):
    return pl.BlockSpec((tm, d), lambda i, j: (i, 0), pipeline_mode=pl.Buffered(1))


def _mod_spec(d, tiles_per_batch, row, chunk):
    return pl.BlockSpec((None, 1, d), lambda i, j: (row + i // tiles_per_batch, 0, chunk))


def _att_prologue(x_ref, gain_ref, sc_ref, sh_ref, h_ref, rs_ref, stage_ref, stage2_ref):
    tm, d = x_ref.shape
    n_chunks = tm // ROW_CHUNK

    _row_rsqrt_to_scratch(x_ref, rs_ref)

    for cg in range(d // (STAGE_COLS * LANES)):
        col = lambda cc: slice((cg * STAGE_COLS + cc) * LANES, (cg * STAGE_COLS + cc + 1) * LANES)

        def natural(i, carry):
            rows = _row_chunk(i, ROW_CHUNK)
            rs = rs_ref[rows, :]
            for cc in range(STAGE_COLS):
                cols = col(cc)
                hn = ((x_ref[rows, cols] * rs * gain_ref[:, cols]) * (1.0 + sc_ref[:, cols])
                      + sh_ref[:, cols])
                h_ref[0, rows, cols] = hn.astype(BF16)
                stage_ref[cc, rows, :] = hn
            return carry

        lax.fori_loop(0, n_chunks, natural, 0)

        dil = DILATED_PATTERNS[1][1]
        per = tm // dil
        sub = DILATED_PATTERNS[2][1] // dil
        per16 = per // sub

        def permute4(r, carry):
            dst = _row_chunk(r, per)
            for cc in range(STAGE_COLS):
                v = stage_ref[cc, pl.ds(r, per, stride=dil), :]
                h_ref[1, dst, col(cc)] = v.astype(BF16)
                stage2_ref[cc, dst, :] = v
            return carry

        lax.fori_loop(0, dil, permute4, 0)

        def permute16(r, carry):
            src = pl.ds((r % dil) * per + r // dil, per16, stride=sub)
            dst = _row_chunk(r, per16)
            for cc in range(STAGE_COLS):
                h_ref[2, dst, col(cc)] = stage2_ref[cc, src, :].astype(BF16)
            return carry

        lax.fori_loop(0, dil * sub, permute16, 0)


def _att_proj_kernel(x_ref, gain_ref, sc_ref, sh_ref, w_ref, hg_ref, *rest, d_model, n_side):
    side_in, rest = rest[:n_side], rest[n_side:]
    o_ref, side_out = rest[0], rest[1:1 + n_side]
    h_ref, rs_ref, stage_ref, stage2_ref = rest[1 + n_side:]
    j = pl.program_id(1)

    @pl.when(j == 0)
    def _():
        _att_prologue(x_ref, gain_ref, sc_ref, sh_ref, h_ref, rs_ref, stage_ref, stage2_ref)

    tn = w_ref.shape[1]
    group = (j * tn) // (3 * d_model)
    is_v = ((j * tn) // d_model) % 3 == 2

    def tile(head_norm):
        _side_cast(side_in, side_out)
        for rc in range(h_ref.shape[1] // MM_ROWS):
            rows = slice(rc * MM_ROWS, (rc + 1) * MM_ROWS)
            acc = jnp.dot(h_ref[group, rows, :], w_ref[...], preferred_element_type=F32)
            if not head_norm:
                o_ref[rows, :] = acc.astype(BF16)
                continue
            for hh in range(tn // HEAD_DIM):
                sl = slice(hh * HEAD_DIM, (hh + 1) * HEAD_DIM)
                a = acc[:, sl]
                ms = jnp.mean(a * a, axis=-1, keepdims=True)
                o_ref[rows, sl] = (a * lax.rsqrt(ms + NORM_EPS) * hg_ref[:, sl]).astype(BF16)

    pl.when(is_v)(functools.partial(tile, False))
    pl.when(jnp.logical_not(is_v))(functools.partial(tile, True))


def _att_proj(x, gain, mod3, mod_row, w, head_gain, tiles_per_batch, side):
    t, d = x.shape
    n = w.shape[1]
    side_specs, side_shapes = _side_cast_specs(side, t // TM, n // TN_WIDE)
    outs = pl.pallas_call(
        functools.partial(_att_proj_kernel, d_model=d, n_side=len(side)),
        grid=(t // TM, n // TN_WIDE),
        in_specs=[_x_tile_spec(TM, d),
                  pl.BlockSpec((1, d), lambda i, j: (0, 0)),
                  _mod_spec(d, tiles_per_batch, mod_row, 1),
                  _mod_spec(d, tiles_per_batch, mod_row, 0),
                  pl.BlockSpec((d, TN_WIDE), lambda i, j: (0, j)),
                  pl.BlockSpec((1, TN_WIDE), lambda i, j: (0, j))] + side_specs,
        out_specs=[pl.BlockSpec((TM, TN_WIDE), lambda i, j: (i, j))] + side_specs,
        out_shape=[jax.ShapeDtypeStruct((t, n), BF16)] + side_shapes,
        scratch_shapes=[pltpu.VMEM((N_GROUPS, TM, d), BF16),
                        pltpu.VMEM((TM, LANES), F32),
                        pltpu.VMEM((STAGE_COLS, TM, LANES), F32),
                        pltpu.VMEM((STAGE_COLS, TM, LANES), F32)],
        compiler_params=_params(2),
        name="att_qkv_proj",
    )(x, gain, mod3, mod3, w, head_gain, *side)
    return outs[0], outs[1:]


def _proj_kernel(x_ref, gain_ref, sc_ref, sh_ref, cos_ref, sin_ref, w_ref, *rest, rot_cols,
                 gate_col0, n_side):
    side_in, rest = rest[:n_side], rest[n_side:]
    o_ref, side_out = rest[0], rest[1:1 + n_side]
    h_ref, rs_ref = rest[1 + n_side:]
    j = pl.program_id(1)

    @pl.when(j == 0)
    def _():
        _norm_mod_to_scratch(x_ref, gain_ref, sc_ref, sh_ref, h_ref, rs_ref)

    tn = w_ref.shape[1]
    half = RET_HEAD_QK // 2

    def tile(kind):
        _side_cast(side_in, side_out)
        for rc in range(h_ref.shape[0] // MM_ROWS):
            rows = slice(rc * MM_ROWS, (rc + 1) * MM_ROWS)
            acc = jnp.dot(h_ref[rows, :], w_ref[...], preferred_element_type=F32)
            if kind == "rotary":
                cos, sin = cos_ref[rows, :], sin_ref[rows, :]
                for hh in range(tn // RET_HEAD_QK):
                    lo = slice(hh * RET_HEAD_QK, hh * RET_HEAD_QK + half)
                    hi = slice(hh * RET_HEAD_QK + half, (hh + 1) * RET_HEAD_QK)
                    o_ref[rows, lo] = (acc[:, lo] * cos - acc[:, hi] * sin).astype(BF16)
                    o_ref[rows, hi] = (acc[:, lo] * sin + acc[:, hi] * cos).astype(BF16)
            else:
                o_ref[rows, :] = (_silu(acc) if kind == "silu" else acc).astype(BF16)

    col0 = j * tn
    pl.when(col0 < rot_cols)(functools.partial(tile, "rotary"))
    pl.when(jnp.logical_and(col0 >= rot_cols, col0 < gate_col0))(functools.partial(tile, "plain"))
    pl.when(col0 >= gate_col0)(functools.partial(tile, "silu"))


def _proj(x, gain, mod3, mod_row, w, seq, cos, sin, rot_cols, gate_col0, side):
    t, d = x.shape
    n = w.shape[1]
    tm = TM
    tiles_per_batch = seq // tm
    side_specs, side_shapes = _side_cast_specs(side, t // tm, n // TN)
    pos_spec = pl.BlockSpec((tm, cos.shape[1]), lambda i, j: (i % tiles_per_batch, 0))
    outs = pl.pallas_call(
        functools.partial(_proj_kernel, rot_cols=rot_cols, gate_col0=gate_col0,
                          n_side=len(side)),
        grid=(t // tm, n // TN),
        in_specs=[pl.BlockSpec((tm, d), lambda i, j: (i, 0)),
                  pl.BlockSpec((1, d), lambda i, j: (0, 0)),
                  _mod_spec(d, tiles_per_batch, mod_row, 1),
                  _mod_spec(d, tiles_per_batch, mod_row, 0),
                  pos_spec, pos_spec,
                  pl.BlockSpec((d, TN), lambda i, j: (0, j))] + side_specs,
        out_specs=[pl.BlockSpec((tm, TN), lambda i, j: (i, j))] + side_specs,
        out_shape=[jax.ShapeDtypeStruct((t, n), BF16)] + side_shapes,
        scratch_shapes=[pltpu.VMEM((tm, d), BF16), pltpu.VMEM((tm, LANES), F32)],
        compiler_params=_params(2),
        name="ret_qkvg_proj",
    )(x, gain, mod3, mod3, cos, sin, w, *side)
    return outs[0], outs[1:]


def _ffn_up_kernel(x_ref, gain_ref, sc_ref, sh_ref, wg_ref, wu_ref, *rest, n_side):
    side_in, rest = rest[:n_side], rest[n_side:]
    o_ref, side_out = rest[0], rest[1:1 + n_side]
    h_ref, rs_ref = rest[1 + n_side:]

    @pl.when(pl.program_id(1) == 0)
    def _():
        _norm_mod_to_scratch(x_ref, gain_ref, sc_ref, sh_ref, h_ref, rs_ref)

    _side_cast(side_in, side_out)
    for rc in range(h_ref.shape[0] // MM_ROWS):
        rows = slice(rc * MM_ROWS, (rc + 1) * MM_ROWS)
        h = h_ref[rows, :]
        gate = jnp.dot(h, wg_ref[...], preferred_element_type=F32)
        up = jnp.dot(h, wu_ref[...], preferred_element_type=F32)
        o_ref[rows, :] = (_silu(gate) * up).astype(BF16)


def _ffn_up(x, gain, mod3, mod_row, w_up, seq, side):
    t, d = x.shape
    d_ff = w_up.shape[1] // 2
    tf = 512
    n_f = d_ff // tf
    tm = TM
    tiles_per_batch = seq // tm
    side_specs, side_shapes = _side_cast_specs(side, t // tm, n_f)
    outs = pl.pallas_call(
        functools.partial(_ffn_up_kernel, n_side=len(side)),
        grid=(t // tm, n_f),
        in_specs=[pl.BlockSpec((tm, d), lambda i, j: (i, 0)),
                  pl.BlockSpec((1, d), lambda i, j: (0, 0)),
                  _mod_spec(d, tiles_per_batch, mod_row, 4),
                  _mod_spec(d, tiles_per_batch, mod_row, 3),
                  pl.BlockSpec((d, tf), lambda i, j: (0, j)),
                  pl.BlockSpec((d, tf), lambda i, j: (0, j + n_f))] + side_specs,
        out_specs=[pl.BlockSpec((tm, tf), lambda i, j: (i, j))] + side_specs,
        out_shape=[jax.ShapeDtypeStruct((t, d_ff), BF16)] + side_shapes,
        scratch_shapes=[pltpu.VMEM((tm, d), BF16), pltpu.VMEM((tm, LANES), F32)],
        compiler_params=_params(2),
        name="ffn_up",
    )(x, gain, mod3, mod3, w_up, w_up, *side)
    return outs[0], outs[1:]


def _out_proj_kernel(a_ref, w_ref, x_ref, g_ref, *rest, n_side):
    side_in, o_ref, side_out = rest[:n_side], rest[n_side], rest[n_side + 1:]
    _side_cast(side_in, side_out)
    for rc in range(a_ref.shape[0] // MM_ROWS):
        rows = slice(rc * MM_ROWS, (rc + 1) * MM_ROWS)
        y = jnp.dot(a_ref[rows, :], w_ref[...], preferred_element_type=F32)
        o_ref[rows, :] = x_ref[rows, :] + g_ref[...] * y


def _gate_spec(n, tn, mod_row, gate_chunk, tiles_per_batch):
    return pl.BlockSpec((None, 1, tn),
                        lambda i, j: (mod_row + i // tiles_per_batch, 0,
                                      gate_chunk * (n // tn) + j))


def _out_proj(a, w, x, mod3, mod_row, gate_chunk, tiles_per_batch, tm, tn, name, side=()):
    t, k = a.shape
    n = w.shape[1]
    side_specs, side_shapes = _side_cast_specs(side, t // tm, n // tn)
    outs = pl.pallas_call(
        functools.partial(_out_proj_kernel, n_side=len(side)),
        grid=(t // tm, n // tn),
        in_specs=[pl.BlockSpec((tm, k), lambda i, j: (i, 0)),
                  pl.BlockSpec((k, tn), lambda i, j: (0, j)),
                  pl.BlockSpec((tm, tn), lambda i, j: (i, j)),
                  _gate_spec(n, tn, mod_row, gate_chunk, tiles_per_batch)] + side_specs,
        out_specs=[pl.BlockSpec((tm, tn), lambda i, j: (i, j))] + side_specs,
        out_shape=[jax.ShapeDtypeStruct((t, n), F32)] + side_shapes,
        compiler_params=_params(2),
        name=name,
    )(a, w, x, mod3, *side)
    return outs[0], outs[1:]


def _bias_kernel(tab_ref, bucket_ref, valid_ref, o_ref):
    g = pl.program_id(0)
    h = pl.program_id(1)
    bucket = bucket_ref[...]
    col = g * ATT_HEADS + h
    acc = jnp.zeros(bucket.shape, F32)
    for b in range(REL_BUCKETS):
        acc = jnp.where(bucket == b, tab_ref[b, col] * LOG2E, acc)
    for first in range(2):
        o_ref[first] = jnp.where(valid_ref[first] != 0, acc, MASK_VALUE)


def _t5_bucket_np(dist):
    max_exact = REL_BUCKETS // 2
    d_f = np.maximum(dist, 1).astype(np.float32)
    large = max_exact + (np.log(d_f / np.float32(max_exact))
                         / np.float32(math.log(REL_MAX_DIST / max_exact))
                         * np.float32(REL_BUCKETS - max_exact)).astype(np.int32)
    large = np.minimum(large, REL_BUCKETS - 1)
    return np.where(dist < max_exact, dist, large).astype(np.int32)


def _bias_tiles(rel_bias):
    blk = ATT_BLK
    qi = np.arange(blk)[:, None]
    ki = np.arange(2 * blk)[None, :]
    delta = blk + qi - ki
    band = (delta >= 0) & (delta <= blk)
    valid = np.stack([band & (ki >= blk), band]).astype(np.int32)
    buckets = np.stack([_t5_bucket_np(np.maximum(delta, 0) * dil)
                        for _, dil in DILATED_PATTERNS])
    return pl.pallas_call(
        _bias_kernel,
        grid=(N_GROUPS, ATT_HEADS),
        in_specs=[pl.BlockSpec(memory_space=pltpu.SMEM),
                  pl.BlockSpec((None, blk, 2 * blk), lambda g, h: (g, 0, 0)),
                  pl.BlockSpec((2, blk, 2 * blk), lambda g, h: (0, 0, 0))],
        out_specs=pl.BlockSpec((None, 2, None, blk, 2 * blk), lambda g, h: (g, 0, h, 0, 0)),
        out_shape=jax.ShapeDtypeStruct((N_GROUPS, 2, ATT_HEADS, blk, 2 * blk), F32),
        compiler_params=_params(2),
        name="rel_bias_tiles",
    )(rel_bias, jnp.asarray(buckets), jnp.asarray(valid))


ATT_PASS = 8


def _att_units(q_ref, kc_ref, kp_ref, vc_ref, vp_ref, bias_ref, store, *, g, heads, first_span):
    blk = ATT_BLK
    dil = DILATED_PATTERNS[g][1]
    dn = (((1,), (1,)), ((), ()))
    ones = jnp.ones((2 * blk, HEAD_DIM), BF16)
    lane = lax.broadcasted_iota(jnp.int32, (blk, LANES), 1)
    head0 = pl.program_id(1) * heads

    def group(units):
        sl = lambda h: slice(h * HEAD_DIM, (h + 1) * HEAD_DIM)
        pairs = [(ui, h) for ui in range(len(units)) for h in range(heads)]
        m_tiles = [jnp.zeros((blk, LANES), F32) for _ in units]
        l_tiles = [jnp.ones((blk, LANES), F32) for _ in units]
        for p0 in range(0, len(pairs), ATT_PASS):
            batch = pairs[p0:p0 + ATT_PASS]
            pieces = []
            for ui, h in batch:
                _, q_of, k_of, _, variant = units[ui]
                pieces.append(lax.dot_general(q_of(sl(h)), k_of(sl(h)), dn,
                                              preferred_element_type=F32)
                              + bias_ref[variant, h])
            s = jnp.concatenate(pieces, axis=0)
            m = jnp.max(jnp.maximum(s[:, :blk], s[:, blk:]), axis=-1, keepdims=True)
            p = jnp.exp2(s - m).astype(BF16)
            for idx, (ui, h) in enumerate(batch):
                u, _, _, v_of, _ = units[ui]
                rows = slice(idx * blk, (idx + 1) * blk)
                v_ext = jnp.concatenate([v_of(sl(h)), ones], axis=1)
                pv = jnp.dot(p[rows], v_ext, preferred_element_type=F32)
                denom = pv[:, HEAD_DIM:]
                o = pv[:, :HEAD_DIM] / denom
                mine = lane == head0 + h
                m_tiles[ui] = jnp.where(mine, m[rows], m_tiles[ui])
                l_tiles[ui] = jnp.where(mine, denom, l_tiles[ui])
                store(u, h, o, None)
        for ui, unit_args in enumerate(units):
            store(unit_args[0], None, None, m_tiles[ui] + jnp.log2(l_tiles[ui]))

    unit = lambda *unit_args: group([unit_args])

    if dil == 1:
        n_units = q_ref.shape[0] // blk
        unit(0,
             lambda sl: q_ref[0:blk, sl],
             lambda sl: jnp.concatenate([kp_ref[:, sl], kc_ref[0:blk, sl]], axis=0),
             lambda sl: jnp.concatenate([vp_ref[:, sl], vc_ref[0:blk, sl]], axis=0),
             jnp.where(first_span, 0, 1))

        def body(u, carry):
            cur = _row_chunk(u, blk)
            both = pl.ds(pl.multiple_of((u - 1) * blk, blk), 2 * blk)
            unit(u, lambda sl: q_ref[cur, sl], lambda sl: kc_ref[both, sl],
                 lambda sl: vc_ref[both, sl], 1)
            return carry

        lax.fori_loop(1, n_units, body, 0)
    else:
        variant = jnp.where(first_span, 0, 1)
        if dil == 4:
            pick = lambda ref, r, sl: ref[r, :, sl]
            prev_cur = lambda pref, cref, r, sl: jnp.concatenate(
                [pref[r, :, sl], cref[r, :, sl]], axis=0)
        else:
            pick = lambda ref, r, sl: jnp.concatenate([ref[0, r, :, sl], ref[1, r, :, sl]], axis=0)
            prev_cur = lambda pref, cref, r, sl: jnp.concatenate(
                [pref[0, r, :, sl], pref[1, r, :, sl], cref[0, r, :, sl], cref[1, r, :, sl]],
                axis=0)

        per_iter = ATT_HEADS // heads

        def body(it, carry):
            def unit_args(r):
                return (r, lambda sl: pick(q_ref, r, sl),
                        lambda sl: prev_cur(kp_ref, kc_ref, r, sl),
                        lambda sl: prev_cur(vp_ref, vc_ref, r, sl), variant)

            group([unit_args(it * per_iter + sub) for sub in range(per_iter)])
            return carry

        lax.fori_loop(0, dil // per_iter, body, 0)


def _att_kernel(q_ref, kc_ref, vc_ref, bias_ref, o_ref, lse_ref, kp_ref, vp_ref, *scratch,
                g, heads):
    blk = ATT_BLK
    dil = DILATED_PATTERNS[g][1]
    first_span = pl.program_id(2) == 0

    @pl.when(first_span)
    def _():
        kp_ref[...] = jnp.zeros_like(kp_ref)
        vp_ref[...] = jnp.zeros_like(vp_ref)

    if dil == 1:
        def store(u, h, o, lse_tile):
            rows = _row_chunk(u, blk)
            if h is None:
                lse_ref[rows, :] = lse_tile
            else:
                o_ref[rows, h * HEAD_DIM:(h + 1) * HEAD_DIM] = o.astype(BF16)
    else:
        o_scr, lse_scr = scratch

        def store(u, h, o, lse_tile):
            rows = pl.ds(u, blk, stride=dil)
            if h is None:
                lse_scr[rows, :] = lse_tile
            else:
                o_scr[h, rows, :] = o

    _att_units(q_ref, kc_ref, kp_ref, vc_ref, vp_ref, bias_ref, store, g=g, heads=heads,
               first_span=first_span)

    if dil == 1:
        last = slice(kc_ref.shape[0] - blk, kc_ref.shape[0])
        kp_ref[...] = kc_ref[last, :]
        vp_ref[...] = vc_ref[last, :]
    else:
        kp_ref[...] = kc_ref[...]
        vp_ref[...] = vc_ref[...]
        for h in range(heads):
            o_ref[:, h * HEAD_DIM:(h + 1) * HEAD_DIM] = o_scr[h].astype(BF16)
        lse_ref[...] = lse_scr[...]


def _attention_group(qkv, bias, g, batch, seq):
    dil = DILATED_PATTERNS[g][1]
    blk = ATT_BLK
    span = dil * blk if dil > 1 else 4 * blk
    heads = ATT_HEADS if dil < 16 else 4
    n_hg = ATT_HEADS // heads
    width = heads * HEAD_DIM
    n_cols = qkv.shape[1]
    col0 = lambda which: (g * 3 + which) * (ATT_HEADS // heads)

    if dil == 1:
        view = qkv.reshape(batch, seq, n_cols)
        cur = lambda which: pl.BlockSpec((None, span, width),
                                         lambda b, h, s: (b, s, col0(which) + h))
        prev_shape = (blk, width)
        scratch = []
    elif dil == 4:
        halves = TM // (dil * blk)
        view = qkv.reshape(batch, seq // TM, dil, halves, blk, n_cols)
        cur = lambda which: pl.BlockSpec(
            (None, None, dil, None, blk, width),
            lambda b, h, s: (b, s // halves, 0, s % halves, 0, col0(which) + h))
        prev_shape = (dil, blk, width)
        scratch = [pltpu.VMEM((heads, span, HEAD_DIM), F32), pltpu.VMEM((span, LANES), F32)]
    else:
        tiles = span // TM
        per = TM // dil
        view = qkv.reshape(batch, seq // span, tiles, dil, per, n_cols)
        cur = lambda which: pl.BlockSpec((None, None, tiles, dil, per, width),
                                         lambda b, h, s: (b, s, 0, 0, 0, col0(which) + h))
        prev_shape = (tiles, dil, per, width)
        scratch = [pltpu.VMEM((heads, span, HEAD_DIM), F32), pltpu.VMEM((span, LANES), F32)]

    o, lse = pl.pallas_call(
        functools.partial(_att_kernel, g=g, heads=heads),
        grid=(batch, n_hg, seq // span),
        in_specs=[cur(0), cur(1), cur(2),
                  pl.BlockSpec((None, 2, heads, blk, 2 * blk), lambda b, h, s: (g, 0, h, 0, 0))],
        out_specs=[pl.BlockSpec((None, span, width), lambda b, h, s: (b, s, h)),
                   pl.BlockSpec((None, None, span, LANES), lambda b, h, s: (b, h, s, 0))],
        out_shape=[jax.ShapeDtypeStruct((batch, seq, ATT_HEADS * HEAD_DIM), BF16),
                   jax.ShapeDtypeStruct((batch, n_hg, seq, LANES), F32)],
        scratch_shapes=[pltpu.VMEM(prev_shape, BF16), pltpu.VMEM(prev_shape, BF16)] + scratch,
        compiler_params=_params(3),
        name=f"dilated_attention_g{g}",
    )(view, view, view, bias)
    return o, lse


MERGE_ROWS = 256


def _att_out_kernel(o0_ref, o1_ref, o2_ref, l0_ref, l1_ref, l2_ref, w_ref, x_ref, g_ref,
                    out_ref):
    o_refs = (o0_ref, o1_ref, o2_ref)
    l_refs = (l0_ref, l1_ref, l2_ref)
    tm = x_ref.shape[0]
    lane = lax.broadcasted_iota(jnp.int32, (MERGE_ROWS, LANES), 1)
    for rc in range(tm // MERGE_ROWS):
        rows = slice(rc * MERGE_ROWS, (rc + 1) * MERGE_ROWS)
        lses = []
        for l_ref in l_refs:
            tile = l_ref[0, rows, :]
            for hg in range(1, l_ref.shape[0]):
                tile = tile + l_ref[hg, rows, :]
            lses.append(tile)
        mx = jnp.maximum(jnp.maximum(lses[0], lses[1]), lses[2])
        es = [jnp.exp2(t - mx) for t in lses]
        den = es[0] + es[1] + es[2]
        ws = [es[0] / den, es[1] / den]
        pieces = []
        for h in range(ATT_HEADS):
            sl = slice(h * HEAD_DIM, (h + 1) * HEAD_DIM)
            base = o_refs[2][rows, sl].astype(F32)
            acc = base
            for w_g, o_ref in zip(ws, o_refs[:2]):
                w_col = jnp.sum(jnp.where(lane == h, w_g, 0.0), axis=-1, keepdims=True)
                acc = acc + w_col * (o_ref[rows, sl].astype(F32) - base)
            pieces.append(acc.astype(BF16))
        merged = jnp.concatenate(pieces, axis=1)
        y = jnp.dot(merged, w_ref[...], preferred_element_type=F32)
        out_ref[rows, :] = x_ref[rows, :] + g_ref[...] * y


def _att_out_proj(outs, lses, w, x, mod3, mod_row, tiles_per_batch_tm, tm):
    t, n = x.shape
    k = w.shape[0]
    seq_tiles = tiles_per_batch_tm
    o_spec = pl.BlockSpec((tm, k), lambda i, j: (i, 0))
    l_spec = lambda n_hg: pl.BlockSpec((None, n_hg, tm, LANES),
                                       lambda i, j: (i // seq_tiles, 0, i % seq_tiles, 0))
    return pl.pallas_call(
        _att_out_kernel,
        grid=(t // tm, 1),
        in_specs=[o_spec, o_spec, o_spec] + [l_spec(l.shape[1]) for l in lses]
                 + [pl.BlockSpec((k, n), lambda i, j: (0, 0)),
                    pl.BlockSpec((tm, n), lambda i, j: (i, 0)),
                    _gate_spec(n, n, mod_row, 2, seq_tiles)],
        out_specs=pl.BlockSpec((tm, n), lambda i, j: (i, 0)),
        out_shape=jax.ShapeDtypeStruct((t, n), F32),
        compiler_params=_params(2),
        name="att_merge_out_proj",
    )(*[o.reshape(t, k) for o in outs], *lses, w, x, mod3)


RET_TOKENS = 1024
RET_HEADS_PER_STEP = 2


def _retention_kernel(q_ref, k_ref, v_ref, g_ref, inner_ref, cross_ref, sdec_ref, cdec_ref,
                      o_ref, state_ref):
    @pl.when(pl.program_id(2) == 0)
    def _():
        state_ref[...] = jnp.zeros_like(state_ref)

    dn_nt = (((1,), (1,)), ((), ()))
    dn_tn = (((0,), (0,)), ((), ()))
    for c in range(RET_TOKENS // RET_CHUNK):
        rows = slice(c * RET_CHUNK, (c + 1) * RET_CHUNK)
        for hh in range(RET_HEADS_PER_STEP):
            qk_cols = slice(hh * RET_HEAD_QK, (hh + 1) * RET_HEAD_QK)
            v_cols = slice(hh * RET_HEAD_V, (hh + 1) * RET_HEAD_V)
            q_b = q_ref[rows, qk_cols]
            k_b = k_ref[rows, qk_cols]
            qc_b = (q_b.astype(F32) * cross_ref[hh]).astype(BF16)
            ks_b = (k_b.astype(F32) * sdec_ref[hh]).astype(BF16)
            v = v_ref[rows, v_cols]

            scores = (lax.dot_general(q_b, k_b, dn_nt, preferred_element_type=F32)
                      * inner_ref[hh])
            state = state_ref[hh]
            o = (jnp.dot(scores.astype(BF16), v, preferred_element_type=F32)
                 + jnp.dot(qc_b, state.astype(BF16), preferred_element_type=F32))
            state_ref[hh] = state * cdec_ref[hh] + lax.dot_general(
                ks_b, v, dn_tn, preferred_element_type=F32)

            ms = jnp.mean(o * o, axis=-1, keepdims=True)
            gate = g_ref[rows, v_cols].astype(F32)
            o_ref[rows, v_cols] = (gate * (o * lax.rsqrt(ms + NORM_EPS))).astype(BF16)


def _rotary_tables(seq):
    half = RET_HEAD_QK // 2
    inv = 1.0 / (10000.0 ** jnp.linspace(0.0, 1.0, half, dtype=F32))
    ang = jnp.arange(seq, dtype=F32)[:, None] * inv[None, :]
    return jnp.cos(ang), jnp.sin(ang)


def _retention(qkvg, batch, seq):
    c = RET_CHUNK
    hq, hv = RET_HEAD_QK, RET_HEAD_V
    hps = RET_HEADS_PER_STEP
    n_qk = RET_HEADS * hq
    n_v = RET_HEADS * hv

    log_gamma = jnp.log(1.0 - 2.0 ** (-5.0 - jnp.arange(RET_HEADS, dtype=F32)))
    pos = jnp.arange(c, dtype=F32)
    diff = pos[:, None] - pos[None, :]
    inner = jnp.where(diff[None] >= 0,
                      jnp.exp(jnp.maximum(diff, 0.0)[None] * log_gamma[:, None, None]), 0.0)
    cross = jnp.exp((pos[None, :] + 1.0) * log_gamma[:, None])
    sdec = jnp.exp((c - 1.0 - pos)[None, :] * log_gamma[:, None])
    cdec = jnp.exp(c * log_gamma)
    k_scale = hq ** -0.5
    inner = inner * k_scale
    cross_l = jnp.broadcast_to(cross[:, :, None], (RET_HEADS, c, hq))
    sdec_l = jnp.broadcast_to((sdec * k_scale)[:, :, None], (RET_HEADS, c, hq))
    cdec_l = jnp.broadcast_to(cdec[:, None, None], (RET_HEADS, 1, hv))

    x = qkvg.reshape(batch, seq, qkvg.shape[1])
    tt = RET_TOKENS
    wq, wv = hps * hq, hps * hv
    head_tab = lambda shape: pl.BlockSpec((hps,) + shape, lambda b, h, s: (h, 0, 0))
    out = pl.pallas_call(
        _retention_kernel,
        grid=(batch, RET_HEADS // hps, seq // tt),
        in_specs=[pl.BlockSpec((None, tt, wq), lambda b, h, s: (b, s, h)),
                  pl.BlockSpec((None, tt, wq), lambda b, h, s: (b, s, n_qk // wq + h)),
                  pl.BlockSpec((None, tt, wv), lambda b, h, s: (b, s, 2 * n_qk // wv + h)),
                  pl.BlockSpec((None, tt, wv), lambda b, h, s: (b, s, (2 * n_qk + n_v) // wv + h)),
                  head_tab((c, c)), head_tab((c, hq)), head_tab((c, hq)), head_tab((1, hv))],
        out_specs=pl.BlockSpec((None, tt, wv), lambda b, h, s: (b, s, h)),
        out_shape=jax.ShapeDtypeStruct((batch, seq, n_v), BF16),
        scratch_shapes=[pltpu.VMEM((hps, hq, hv), F32)],
        compiler_params=_params(3),
        name="retention",
    )(x, x, x, x, inner, cross_l, sdec_l, cdec_l)
    return out.reshape(batch * seq, n_v)


def kernel(x, c, w_mod, b_mod, norm_mix, norm_ffn, rel_bias, att_w_qkv, att_q_gain, att_k_gain,
           att_w_o, ret_w_qkvg, ret_w_o, ffn_w_up, ffn_w_down):
    batch, seq, d = x.shape
    depth = w_mod.shape[0]
    assert seq % (2 * TM) == 0 and d == ATT_HEADS * HEAD_DIM
    tiles_per_batch = seq // TM

    mod = _modulation(c, w_mod, b_mod)
    mod3 = mod.reshape(depth * batch, 1, 6 * d)
    bias = _bias_tiles(rel_bias)
    xf = x.reshape(batch * seq, d)

    ready = {}

    def bf16_weight(name, layer, w):
        return ready.pop((name, layer)) if (name, layer) in ready else w.astype(BF16)

    def first_weight(layer):
        if layer >= depth:
            return None
        return (("att_w_qkv", att_w_qkv[layer // 2]) if layer % 2 == 0
                else ("ret_w_qkvg", ret_w_qkvg[layer // 2]))

    for i in range(depth):
        jdx = i // 2
        mod_row = i * batch
        gain_mix = norm_mix[i].reshape(1, d)
        if i % 2 == 0:
            scale = HEAD_DIM ** -0.5 * LOG2E
            ones = jnp.ones((ATT_HEADS * HEAD_DIM,), F32)
            head_gain = jnp.concatenate(
                [jnp.concatenate([jnp.tile(att_q_gain[jdx, g] * scale, ATT_HEADS),
                                  jnp.tile(att_k_gain[jdx, g], ATT_HEADS), ones])
                 for g in range(N_GROUPS)]).reshape(1, -1)
            qkv, (w_o, w_up) = _att_proj(xf, gain_mix, mod3, mod_row,
                                         bf16_weight("att_w_qkv", i, att_w_qkv[jdx]), head_gain,
                                         tiles_per_batch, [att_w_o[jdx], ffn_w_up[i]])
            ready[("ffn_w_up", i)] = w_up
            outs, lses = zip(*[_attention_group(qkv, bias, g, batch, seq)
                               for g in range(N_GROUPS)])
            att_tm = 512
            xf = _att_out_proj(outs, lses, w_o, xf, mod3, mod_row, seq // att_tm, att_tm)
        else:
            n_qk = RET_HEADS * RET_HEAD_QK
            qkvg, (w_o, w_up) = _proj(xf, gain_mix, mod3, mod_row,
                                      bf16_weight("ret_w_qkvg", i, ret_w_qkvg[jdx]), seq,
                                      *_rotary_tables(seq), 2 * n_qk,
                                      2 * n_qk + RET_HEADS * RET_HEAD_V,
                                      [ret_w_o[jdx], ffn_w_up[i]])
            ready[("ffn_w_up", i)] = w_up
            mixed = _retention(qkvg, batch, seq)
            xf, _ = _out_proj(mixed, w_o, xf, mod3, mod_row, 2, tiles_per_batch, TM, TN,
                              "ret_out_proj")

        act, (w_down,) = _ffn_up(xf, norm_ffn[i].reshape(1, d), mod3, mod_row,
                                 bf16_weight("ffn_w_up", i, ffn_w_up[i]), seq, [ffn_w_down[i]])
        nxt = first_weight(i + 1)
        xf, cast = _out_proj(act, w_down, xf, mod3, mod_row, 5, tiles_per_batch, TM, 512,
                             "ffn_down", [nxt[1]] if nxt else [])
        if nxt:
            ready[(nxt[0], i + 1)] = cast[0]
    return xf.reshape(batch, seq, d)
```

```python
import functools
import math

import numpy as np
import jax
import jax.numpy as jnp
from jax import lax
from jax.experimental import pallas as pl
from jax.experimental.pallas import tpu as pltpu

F32 = jnp.float32
BF16 = jnp.bfloat16

NORM_EPS = 1e-6
MASK_VALUE = -1e30
LOG2E = math.log2(math.e)
LANES = 128

ATT_HEADS = 16
HEAD_DIM = 128
DILATED_PATTERNS = ((128, 1), (512, 4), (2048, 16))
N_GROUPS = len(DILATED_PATTERNS)
ATT_BLK = 128
REL_BUCKETS = 32
REL_MAX_DIST = 2048

RET_HEADS = 8
RET_HEAD_QK = 256
RET_HEAD_V = 512
RET_CHUNK = 256

VMEM_LIMIT_BYTES = 56 * 1024 * 1024

TM = 1024
TN = 1024
TN_WIDE = 2048
ROW_CHUNK = 256
MM_ROWS = 256
STAGE_COLS = 4
CAST_ROWS = 16


def _params(n_axes):
    return pltpu.CompilerParams(dimension_semantics=("arbitrary",) * n_axes,
                                vmem_limit_bytes=VMEM_LIMIT_BYTES)


def _silu(v):
    return v / (1.0 + jnp.exp(-v))


def _row_chunk(i, size):
    if isinstance(i, int):
        return pl.ds(i * size, size)
    return pl.ds(pl.multiple_of(i * size, size), size)


def _mod_kernel(c_ref, w_ref, b_ref, o_ref):
    cond = _silu(c_ref[...]).astype(BF16)
    o_ref[...] = (jnp.dot(cond, w_ref[...].astype(BF16), preferred_element_type=F32)
                  + b_ref[...])


def _modulation(c, w_mod, b_mod):
    depth, d, n = w_mod.shape
    b = c.shape[0]
    tn = 1024
    return pl.pallas_call(
        _mod_kernel,
        grid=(depth, n // tn),
        in_specs=[pl.BlockSpec((b, d), lambda l, j: (0, 0)),
                  pl.BlockSpec((None, d, tn), lambda l, j: (l, 0, j)),
                  pl.BlockSpec((None, 1, tn), lambda l, j: (l, 0, j))],
        out_specs=pl.BlockSpec((None, b, tn), lambda l, j: (l, 0, j)),
        out_shape=jax.ShapeDtypeStruct((depth, b, n), F32),
        compiler_params=_params(2),
        name="adaln_mod",
    )(c, w_mod, b_mod.reshape(depth, 1, n))


def _row_rsqrt_to_scratch(x_ref, rs_ref):
    def body(i, carry):
        rows = _row_chunk(i, ROW_CHUNK)
        xx = x_ref[rows, :]
        ms = jnp.mean(xx * xx, axis=-1, keepdims=True)
        rs_ref[rows, :] = jnp.broadcast_to(lax.rsqrt(ms + NORM_EPS), (ROW_CHUNK, LANES))
        return carry

    lax.fori_loop(0, x_ref.shape[0] // ROW_CHUNK, body, 0)


def _norm_mod_to_scratch(x_ref, gain_ref, sc_ref, sh_ref, h_ref, rs_ref):
    _row_rsqrt_to_scratch(x_ref, rs_ref)

    def body(i, carry):
        rows = _row_chunk(i, ROW_CHUNK)
        rs = rs_ref[rows, :]
        for cb in range(x_ref.shape[1] // LANES):
            cols = slice(cb * LANES, (cb + 1) * LANES)
            hn = ((x_ref[rows, cols] * rs * gain_ref[:, cols]) * (1.0 + sc_ref[:, cols])
                  + sh_ref[:, cols])
            h_ref[rows, cols] = hn.astype(BF16)
        return carry

    lax.fori_loop(0, x_ref.shape[0] // ROW_CHUNK, body, 0)


def _side_cast_specs(side, n_row_steps, n_cols_steps):
    in_specs, out_specs, shapes = [], [], []
    for w, layer in side:
        _, rows, cols = w.shape
        rb = CAST_ROWS
        while rows // rb > n_row_steps * n_cols_steps:
            rb *= 2
        assert rows % rb == 0
        block = lambda i, j, last=rows // rb - 1: jnp.minimum(i * n_cols_steps + j, last)
        in_specs.append(pl.BlockSpec((None, rb, cols),
                                     lambda i, j, b=block, l=layer: (l, b(i, j), 0)))
        out_specs.append(pl.BlockSpec((rb, cols), lambda i, j, b=block: (b(i, j), 0)))
        shapes.append(jax.ShapeDtypeStruct((rows, cols), BF16))
    return in_specs, out_specs, shapes


def _side_cast(src_refs, dst_refs):
    for src, dst in zip(src_refs, dst_refs):
        dst[...] = src[...].astype(BF16)


def _x_tile_spec(tm, d):
    return pl.BlockSpec((tm, d), lambda i, j: (i, 0), pipeline_mode=pl.Buffered(1))


def _mod_spec(d, tiles_per_batch, row, chunk):
    return pl.BlockSpec((None, 1, d), lambda i, j: (row + i // tiles_per_batch, 0, chunk))


def _att_prologue(x_ref, gain_ref, sc_ref, sh_ref, h_ref, rs_ref, stage_ref, stage2_ref):
    tm, d = x_ref.shape
    n_chunks = tm // ROW_CHUNK

    _row_rsqrt_to_scratch(x_ref, rs_ref)

    for cg in range(d // (STAGE_COLS * LANES)):
        col = lambda cc: slice((cg * STAGE_COLS + cc) * LANES, (cg * STAGE_COLS + cc + 1) * LANES)

        def natural(i, carry):
            rows = _row_chunk(i, ROW_CHUNK)
            rs = rs_ref[rows, :]
            for cc in range(STAGE_COLS):
                cols = col(cc)
                hn = ((x_ref[rows, cols] * rs * gain_ref[:, cols]) * (1.0 + sc_ref[:, cols])
                      + sh_ref[:, cols])
                h_ref[0, rows, cols] = hn.astype(BF16)
                stage_ref[cc, rows, :] = hn
            return carry

        lax.fori_loop(0, n_chunks, natural, 0)

        dil = DILATED_PATTERNS[1][1]
        per = tm // dil
        sub = DILATED_PATTERNS[2][1] // dil
        per16 = per // sub

        def permute4(r, carry):
            dst = _row_chunk(r, per)
            for cc in range(STAGE_COLS):
                v = stage_ref[cc, pl.ds(r, per, stride=dil), :]
                h_ref[1, dst, col(cc)] = v.astype(BF16)
                stage2_ref[cc, dst, :] = v
            return carry

        lax.fori_loop(0, dil, permute4, 0)

        def permute16(r, carry):
            src = pl.ds((r % dil) * per + r // dil, per16, stride=sub)
            dst = _row_chunk(r, per16)
            for cc in range(STAGE_COLS):
                h_ref[2, dst, col(cc)] = stage2_ref[cc, src, :].astype(BF16)
            return carry

        lax.fori_loop(0, dil * sub, permute16, 0)


def _att_proj_kernel(x_ref, gain_ref, sc_ref, sh_ref, w_ref, hg_ref, *rest, d_model, n_side):
    side_in, rest = rest[:n_side], rest[n_side:]
    o_ref, side_out = rest[0], rest[1:1 + n_side]
    h_ref, rs_ref, stage_ref, stage2_ref = rest[1 + n_side:]
    j = pl.program_id(1)

    @pl.when(j == 0)
    def _():
        _att_prologue(x_ref, gain_ref, sc_ref, sh_ref, h_ref, rs_ref, stage_ref, stage2_ref)

    tn = w_ref.shape[1]
    group = (j * tn) // (3 * d_model)
    is_v = ((j * tn) // d_model) % 3 == 2

    def tile(head_norm):
        _side_cast(side_in, side_out)
        for rc in range(h_ref.shape[1] // MM_ROWS):
            rows = slice(rc * MM_ROWS, (rc + 1) * MM_ROWS)
            acc = jnp.dot(h_ref[group, rows, :], w_ref[...], preferred_element_type=F32)
            if not head_norm:
                o_ref[rows, :] = acc.astype(BF16)
                continue
            for hh in range(tn // HEAD_DIM):
                sl = slice(hh * HEAD_DIM, (hh + 1) * HEAD_DIM)
                a = acc[:, sl]
                ms = jnp.mean(a * a, axis=-1, keepdims=True)
                o_ref[rows, sl] = (a * lax.rsqrt(ms + NORM_EPS) * hg_ref[:, sl]).astype(BF16)

    pl.when(is_v)(functools.partial(tile, False))
    pl.when(jnp.logical_not(is_v))(functools.partial(tile, True))


def _att_proj(x, gain, mod3, mod_row, w, head_gain, tiles_per_batch, side):
    t, d = x.shape
    n = w.shape[1]
    side_in, side_out, side_shapes = _side_cast_specs(side, t // TM, n // TN_WIDE)
    outs = pl.pallas_call(
        functools.partial(_att_proj_kernel, d_model=d, n_side=len(side)),
        grid=(t // TM, n // TN_WIDE),
        in_specs=[_x_tile_spec(TM, d),
                  pl.BlockSpec((1, d), lambda i, j: (0, 0)),
                  _mod_spec(d, tiles_per_batch, mod_row, 1),
                  _mod_spec(d, tiles_per_batch, mod_row, 0),
                  pl.BlockSpec((d, TN_WIDE), lambda i, j: (0, j)),
                  pl.BlockSpec((1, TN_WIDE), lambda i, j: (0, j))] + side_in,
        out_specs=[pl.BlockSpec((TM, TN_WIDE), lambda i, j: (i, j))] + side_out,
        out_shape=[jax.ShapeDtypeStruct((t, n), BF16)] + side_shapes,
        scratch_shapes=[pltpu.VMEM((N_GROUPS, TM, d), BF16),
                        pltpu.VMEM((TM, LANES), F32),
                        pltpu.VMEM((STAGE_COLS, TM, LANES), F32),
                        pltpu.VMEM((STAGE_COLS, TM, LANES), F32)],
        compiler_params=_params(2),
        name="att_qkv_proj",
    )(x, gain, mod3, mod3, w, head_gain, *[w for w, _ in side])
    return outs[0], outs[1:]


def _proj_kernel(x_ref, gain_ref, sc_ref, sh_ref, cos_ref, sin_ref, w_ref, *rest, rot_cols,
                 gate_col0, n_side):
    side_in, rest = rest[:n_side], rest[n_side:]
    o_ref, side_out = rest[0], rest[1:1 + n_side]
    h_ref, rs_ref = rest[1 + n_side:]
    j = pl.program_id(1)

    @pl.when(j == 0)
    def _():
        _norm_mod_to_scratch(x_ref, gain_ref, sc_ref, sh_ref, h_ref, rs_ref)

    tn = w_ref.shape[1]
    half = RET_HEAD_QK // 2

    def tile(kind):
        _side_cast(side_in, side_out)
        for rc in range(h_ref.shape[0] // MM_ROWS):
            rows = slice(rc * MM_ROWS, (rc + 1) * MM_ROWS)
            acc = jnp.dot(h_ref[rows, :], w_ref[...], preferred_element_type=F32)
            if kind == "rotary":
                cos, sin = cos_ref[rows, :], sin_ref[rows, :]
                for hh in range(tn // RET_HEAD_QK):
                    lo = slice(hh * RET_HEAD_QK, hh * RET_HEAD_QK + half)
                    hi = slice(hh * RET_HEAD_QK + half, (hh + 1) * RET_HEAD_QK)
                    o_ref[rows, lo] = (acc[:, lo] * cos - acc[:, hi] * sin).astype(BF16)
                    o_ref[rows, hi] = (acc[:, lo] * sin + acc[:, hi] * cos).astype(BF16)
            else:
                o_ref[rows, :] = (_silu(acc) if kind == "silu" else acc).astype(BF16)

    col0 = j * tn
    pl.when(col0 < rot_cols)(functools.partial(tile, "rotary"))
    pl.when(jnp.logical_and(col0 >= rot_cols, col0 < gate_col0))(functools.partial(tile, "plain"))
    pl.when(col0 >= gate_col0)(functools.partial(tile, "silu"))


def _proj(x, gain, mod3, mod_row, w, seq, cos, sin, rot_cols, gate_col0, side):
    t, d = x.shape
    n = w.shape[1]
    tm = TM
    tiles_per_batch = seq // tm
    side_in, side_out, side_shapes = _side_cast_specs(side, t // tm, n // TN)
    pos_spec = pl.BlockSpec((tm, cos.shape[1]), lambda i, j: (i % tiles_per_batch, 0))
    outs = pl.pallas_call(
        functools.partial(_proj_kernel, rot_cols=rot_cols, gate_col0=gate_col0,
                          n_side=len(side)),
        grid=(t // tm, n // TN),
        in_specs=[pl.BlockSpec((tm, d), lambda i, j: (i, 0)),
                  pl.BlockSpec((1, d), lambda i, j: (0, 0)),
                  _mod_spec(d, tiles_per_batch, mod_row, 1),
                  _mod_spec(d, tiles_per_batch, mod_row, 0),
                  pos_spec, pos_spec,
                  pl.BlockSpec((d, TN), lambda i, j: (0, j))] + side_in,
        out_specs=[pl.BlockSpec((tm, TN), lambda i, j: (i, j))] + side_out,
        out_shape=[jax.ShapeDtypeStruct((t, n), BF16)] + side_shapes,
        scratch_shapes=[pltpu.VMEM((tm, d), BF16), pltpu.VMEM((tm, LANES), F32)],
        compiler_params=_params(2),
        name="ret_qkvg_proj",
    )(x, gain, mod3, mod3, cos, sin, w, *[w for w, _ in side])
    return outs[0], outs[1:]


def _ffn_up_kernel(x_ref, gain_ref, sc_ref, sh_ref, wg_ref, wu_ref, *rest, n_side):
    side_in, rest = rest[:n_side], rest[n_side:]
    o_ref, side_out = rest[0], rest[1:1 + n_side]
    h_ref, rs_ref = rest[1 + n_side:]

    @pl.when(pl.program_id(1) == 0)
    def _():
        _norm_mod_to_scratch(x_ref, gain_ref, sc_ref, sh_ref, h_ref, rs_ref)

    _side_cast(side_in, side_out)
    for rc in range(h_ref.shape[0] // MM_ROWS):
        rows = slice(rc * MM_ROWS, (rc + 1) * MM_ROWS)
        h = h_ref[rows, :]
        gate = jnp.dot(h, wg_ref[...], preferred_element_type=F32)
        up = jnp.dot(h, wu_ref[...], preferred_element_type=F32)
        o_ref[rows, :] = (_silu(gate) * up).astype(BF16)


def _ffn_up(x, gain, mod3, mod_row, w_up, seq, side):
    t, d = x.shape
    d_ff = w_up.shape[1] // 2
    tf = 512
    n_f = d_ff // tf
    tm = TM
    tiles_per_batch = seq // tm
    side_in, side_out, side_shapes = _side_cast_specs(side, t // tm, n_f)
    outs = pl.pallas_call(
        functools.partial(_ffn_up_kernel, n_side=len(side)),
        grid=(t // tm, n_f),
        in_specs=[pl.BlockSpec((tm, d), lambda i, j: (i, 0)),
                  pl.BlockSpec((1, d), lambda i, j: (0, 0)),
                  _mod_spec(d, tiles_per_batch, mod_row, 4),
                  _mod_spec(d, tiles_per_batch, mod_row, 3),
                  pl.BlockSpec((d, tf), lambda i, j: (0, j)),
                  pl.BlockSpec((d, tf), lambda i, j: (0, j + n_f))] + side_in,
        out_specs=[pl.BlockSpec((tm, tf), lambda i, j: (i, j))] + side_out,
        out_shape=[jax.ShapeDtypeStruct((t, d_ff), BF16)] + side_shapes,
        scratch_shapes=[pltpu.VMEM((tm, d), BF16), pltpu.VMEM((tm, LANES), F32)],
        compiler_params=_params(2),
        name="ffn_up",
    )(x, gain, mod3, mod3, w_up, w_up, *[w for w, _ in side])
    return outs[0], outs[1:]


def _out_proj_kernel(a_ref, w_ref, x_ref, g_ref, *rest, n_side):
    side_in, o_ref, side_out = rest[:n_side], rest[n_side], rest[n_side + 1:]
    _side_cast(side_in, side_out)
    for rc in range(a_ref.shape[0] // MM_ROWS):
        rows = slice(rc * MM_ROWS, (rc + 1) * MM_ROWS)
        y = jnp.dot(a_ref[rows, :], w_ref[...], preferred_element_type=F32)
        o_ref[rows, :] = x_ref[rows, :] + g_ref[...] * y


def _gate_spec(n, tn, mod_row, gate_chunk, tiles_per_batch):
    return pl.BlockSpec((None, 1, tn),
                        lambda i, j: (mod_row + i // tiles_per_batch, 0,
                                      gate_chunk * (n // tn) + j))


def _out_proj(a, w, x, mod3, mod_row, gate_chunk, tiles_per_batch, tm, tn, name, side=()):
    t, k = a.shape
    n = w.shape[1]
    side_in, side_out, side_shapes = _side_cast_specs(side, t // tm, n // tn)
    outs = pl.pallas_call(
        functools.partial(_out_proj_kernel, n_side=len(side)),
        grid=(t // tm, n // tn),
        in_specs=[pl.BlockSpec((tm, k), lambda i, j: (i, 0)),
                  pl.BlockSpec((k, tn), lambda i, j: (0, j)),
                  pl.BlockSpec((tm, tn), lambda i, j: (i, j)),
                  _gate_spec(n, tn, mod_row, gate_chunk, tiles_per_batch)] + side_in,
        out_specs=[pl.BlockSpec((tm, tn), lambda i, j: (i, j))] + side_out,
        out_shape=[jax.ShapeDtypeStruct((t, n), F32)] + side_shapes,
        compiler_params=_params(2),
        name=name,
    )(a, w, x, mod3, *[w for w, _ in side])
    return outs[0], outs[1:]


def _bias_kernel(tab_ref, bucket_ref, valid_ref, o_ref):
    g = pl.program_id(0)
    h = pl.program_id(1)
    bucket = bucket_ref[...]
    col = g * ATT_HEADS + h
    acc = jnp.zeros(bucket.shape, F32)
    for b in range(REL_BUCKETS):
        acc = jnp.where(bucket == b, tab_ref[b, col] * LOG2E, acc)
    for first in range(2):
        o_ref[first] = jnp.where(valid_ref[first] != 0, acc, MASK_VALUE)


def _t5_bucket_np(dist):
    max_exact = REL_BUCKETS // 2
    d_f = np.maximum(dist, 1).astype(np.float32)
    large = max_exact + (np.log(d_f / np.float32(max_exact))
                         / np.float32(math.log(REL_MAX_DIST / max_exact))
                         * np.float32(REL_BUCKETS - max_exact)).astype(np.int32)
    large = np.minimum(large, REL_BUCKETS - 1)
    return np.where(dist < max_exact, dist, large).astype(np.int32)


def _bias_tiles(rel_bias):
    blk = ATT_BLK
    qi = np.arange(blk)[:, None]
    ki = np.arange(2 * blk)[None, :]
    delta = blk + qi - ki
    band = (delta >= 0) & (delta <= blk)
    valid = np.stack([band & (ki >= blk), band]).astype(np.int32)
    buckets = np.stack([_t5_bucket_np(np.maximum(delta, 0) * dil)
                        for _, dil in DILATED_PATTERNS])
    return pl.pallas_call(
        _bias_kernel,
        grid=(N_GROUPS, ATT_HEADS),
        in_specs=[pl.BlockSpec(memory_space=pltpu.SMEM),
                  pl.BlockSpec((None, blk, 2 * blk), lambda g, h: (g, 0, 0)),
                  pl.BlockSpec((2, blk, 2 * blk), lambda g, h: (0, 0, 0))],
        out_specs=pl.BlockSpec((None, 2, None, blk, 2 * blk), lambda g, h: (g, 0, h, 0, 0)),
        out_shape=jax.ShapeDtypeStruct((N_GROUPS, 2, ATT_HEADS, blk, 2 * blk), F32),
        compiler_params=_params(2),
        name="rel_bias_tiles",
    )(rel_bias, jnp.asarray(buckets), jnp.asarray(valid))


ATT_PASS = 8


def _att_units(q_ref, kc_ref, kp_ref, vc_ref, vp_ref, bias_ref, store, *, g, heads, first_span):
    blk = ATT_BLK
    dil = DILATED_PATTERNS[g][1]
    dn = (((1,), (1,)), ((), ()))
    ones = jnp.ones((2 * blk, HEAD_DIM), BF16)
    lane = lax.broadcasted_iota(jnp.int32, (blk, LANES), 1)
    head0 = pl.program_id(1) * heads

    def group(units):
        sl = lambda h: slice(h * HEAD_DIM, (h + 1) * HEAD_DIM)
        pairs = [(ui, h) for ui in range(len(units)) for h in range(heads)]
        m_tiles = [jnp.zeros((blk, LANES), F32) for _ in units]
        l_tiles = [jnp.ones((blk, LANES), F32) for _ in units]
        for p0 in range(0, len(pairs), ATT_PASS):
            batch = pairs[p0:p0 + ATT_PASS]
            pieces = []
            for ui, h in batch:
                _, q_of, k_of, _, variant = units[ui]
                pieces.append(lax.dot_general(q_of(sl(h)), k_of(sl(h)), dn,
                                              preferred_element_type=F32)
                              + bias_ref[variant, h])
            s = jnp.concatenate(pieces, axis=0)
            m = jnp.max(jnp.maximum(s[:, :blk], s[:, blk:]), axis=-1, keepdims=True)
            p = jnp.exp2(s - m).astype(BF16)
            for idx, (ui, h) in enumerate(batch):
                u, _, _, v_of, _ = units[ui]
                rows = slice(idx * blk, (idx + 1) * blk)
                v_ext = jnp.concatenate([v_of(sl(h)), ones], axis=1)
                pv = jnp.dot(p[rows], v_ext, preferred_element_type=F32)
                denom = pv[:, HEAD_DIM:]
                o = pv[:, :HEAD_DIM] / denom
                mine = lane == head0 + h
                m_tiles[ui] = jnp.where(mine, m[rows], m_tiles[ui])
                l_tiles[ui] = jnp.where(mine, denom, l_tiles[ui])
                store(u, h, o, None)
        for ui, unit_args in enumerate(units):
            store(unit_args[0], None, None, m_tiles[ui] + jnp.log2(l_tiles[ui]))

    unit = lambda *unit_args: group([unit_args])

    if dil == 1:
        n_units = q_ref.shape[0] // blk
        unit(0,
             lambda sl: q_ref[0:blk, sl],
             lambda sl: jnp.concatenate([kp_ref[:, sl], kc_ref[0:blk, sl]], axis=0),
             lambda sl: jnp.concatenate([vp_ref[:, sl], vc_ref[0:blk, sl]], axis=0),
             jnp.where(first_span, 0, 1))

        def body(u, carry):
            cur = _row_chunk(u, blk)
            both = pl.ds(pl.multiple_of((u - 1) * blk, blk), 2 * blk)
            unit(u, lambda sl: q_ref[cur, sl], lambda sl: kc_ref[both, sl],
                 lambda sl: vc_ref[both, sl], 1)
            return carry

        lax.fori_loop(1, n_units, body, 0)
    else:
        variant = jnp.where(first_span, 0, 1)
        if dil == 4:
            pick = lambda ref, r, sl: ref[r, :, sl]
            prev_cur = lambda pref, cref, r, sl: jnp.concatenate(
                [pref[r, :, sl], cref[r, :, sl]], axis=0)
        else:
            pick = lambda ref, r, sl: jnp.concatenate([ref[0, r, :, sl], ref[1, r, :, sl]], axis=0)
            prev_cur = lambda pref, cref, r, sl: jnp.concatenate(
                [pref[0, r, :, sl], pref[1, r, :, sl], cref[0, r, :, sl], cref[1, r, :, sl]],
                axis=0)

        per_iter = ATT_HEADS // heads

        def body(it, carry):
            def unit_args(r):
                return (r, lambda sl: pick(q_ref, r, sl),
                        lambda sl: prev_cur(kp_ref, kc_ref, r, sl),
                        lambda sl: prev_cur(vp_ref, vc_ref, r, sl), variant)

            group([unit_args(it * per_iter + sub) for sub in range(per_iter)])
            return carry

        lax.fori_loop(0, dil // per_iter, body, 0)


def _att_kernel(q_ref, kc_ref, vc_ref, bias_ref, o_ref, lse_ref, kp_ref, vp_ref, *scratch,
                g, heads):
    blk = ATT_BLK
    dil = DILATED_PATTERNS[g][1]
    first_span = pl.program_id(2) == 0

    @pl.when(first_span)
    def _():
        kp_ref[...] = jnp.zeros_like(kp_ref)
        vp_ref[...] = jnp.zeros_like(vp_ref)

    if dil == 1:
        def store(u, h, o, lse_tile):
            rows = _row_chunk(u, blk)
            if h is None:
                lse_ref[rows, :] = lse_tile
            else:
                o_ref[rows, h * HEAD_DIM:(h + 1) * HEAD_DIM] = o.astype(BF16)
    else:
        o_scr, lse_scr = scratch

        def store(u, h, o, lse_tile):
            rows = pl.ds(u, blk, stride=dil)
            if h is None:
                lse_scr[rows, :] = lse_tile
            else:
                o_scr[h, rows, :] = o

    _att_units(q_ref, kc_ref, kp_ref, vc_ref, vp_ref, bias_ref, store, g=g, heads=heads,
               first_span=first_span)

    if dil == 1:
        last = slice(kc_ref.shape[0] - blk, kc_ref.shape[0])
        kp_ref[...] = kc_ref[last, :]
        vp_ref[...] = vc_ref[last, :]
    else:
        kp_ref[...] = kc_ref[...]
        vp_ref[...] = vc_ref[...]
        for h in range(heads):
            o_ref[:, h * HEAD_DIM:(h + 1) * HEAD_DIM] = o_scr[h].astype(BF16)
        lse_ref[...] = lse_scr[...]


def _attention_group(qkv, bias, g, batch, seq):
    dil = DILATED_PATTERNS[g][1]
    blk = ATT_BLK
    span = dil * blk if dil > 1 else 4 * blk
    heads = ATT_HEADS if dil < 16 else 4
    n_hg = ATT_HEADS // heads
    width = heads * HEAD_DIM
    n_cols = qkv.shape[1]
    col0 = lambda which: (g * 3 + which) * (ATT_HEADS // heads)

    if dil == 1:
        view = qkv.reshape(batch, seq, n_cols)
        cur = lambda which: pl.BlockSpec((None, span, width),
                                         lambda b, h, s: (b, s, col0(which) + h))
        prev_shape = (blk, width)
        scratch = []
    elif dil == 4:
        halves = TM // (dil * blk)
        view = qkv.reshape(batch, seq // TM, dil, halves, blk, n_cols)
        cur = lambda which: pl.BlockSpec(
            (None, None, dil, None, blk, width),
            lambda b, h, s: (b, s // halves, 0, s % halves, 0, col0(which) + h))
        prev_shape = (dil, blk, width)
        scratch = [pltpu.VMEM((heads, span, HEAD_DIM), F32), pltpu.VMEM((span, LANES), F32)]
    else:
        tiles = span // TM
        per = TM // dil
        view = qkv.reshape(batch, seq // span, tiles, dil, per, n_cols)
        cur = lambda which: pl.BlockSpec((None, None, tiles, dil, per, width),
                                         lambda b, h, s: (b, s, 0, 0, 0, col0(which) + h))
        prev_shape = (tiles, dil, per, width)
        scratch = [pltpu.VMEM((heads, span, HEAD_DIM), F32), pltpu.VMEM((span, LANES), F32)]

    o, lse = pl.pallas_call(
        functools.partial(_att_kernel, g=g, heads=heads),
        grid=(batch, n_hg, seq // span),
        in_specs=[cur(0), cur(1), cur(2),
                  pl.BlockSpec((None, 2, heads, blk, 2 * blk), lambda b, h, s: (g, 0, h, 0, 0))],
        out_specs=[pl.BlockSpec((None, span, width), lambda b, h, s: (b, s, h)),
                   pl.BlockSpec((None, None, span, LANES), lambda b, h, s: (b, h, s, 0))],
        out_shape=[jax.ShapeDtypeStruct((batch, seq, ATT_HEADS * HEAD_DIM), BF16),
                   jax.ShapeDtypeStruct((batch, n_hg, seq, LANES), F32)],
        scratch_shapes=[pltpu.VMEM(prev_shape, BF16), pltpu.VMEM(prev_shape, BF16)] + scratch,
        compiler_params=_params(3),
        name=f"dilated_attention_g{g}",
    )(view, view, view, bias)
    return o, lse


MERGE_ROWS = 256


def _att_out_kernel(o0_ref, o1_ref, o2_ref, l0_ref, l1_ref, l2_ref, w_ref, x_ref, g_ref,
                    out_ref):
    o_refs = (o0_ref, o1_ref, o2_ref)
    l_refs = (l0_ref, l1_ref, l2_ref)
    tm = x_ref.shape[0]
    lane = lax.broadcasted_iota(jnp.int32, (MERGE_ROWS, LANES), 1)
    for rc in range(tm // MERGE_ROWS):
        rows = slice(rc * MERGE_ROWS, (rc + 1) * MERGE_ROWS)
        lses = []
        for l_ref in l_refs:
            tile = l_ref[0, rows, :]
            for hg in range(1, l_ref.shape[0]):
                tile = tile + l_ref[hg, rows, :]
            lses.append(tile)
        mx = jnp.maximum(jnp.maximum(lses[0], lses[1]), lses[2])
        es = [jnp.exp2(t - mx) for t in lses]
        den = es[0] + es[1] + es[2]
        ws = [es[0] / den, es[1] / den]
        pieces = []
        for h in range(ATT_HEADS):
            sl = slice(h * HEAD_DIM, (h + 1) * HEAD_DIM)
            base = o_refs[2][rows, sl].astype(F32)
            acc = base
            for w_g, o_ref in zip(ws, o_refs[:2]):
                w_col = jnp.sum(jnp.where(lane == h, w_g, 0.0), axis=-1, keepdims=True)
                acc = acc + w_col * (o_ref[rows, sl].astype(F32) - base)
            pieces.append(acc.astype(BF16))
        merged = jnp.concatenate(pieces, axis=1)
        y = jnp.dot(merged, w_ref[...], preferred_element_type=F32)
        out_ref[rows, :] = x_ref[rows, :] + g_ref[...] * y


def _att_out_proj(outs, lses, w, x, mod3, mod_row, tiles_per_batch_tm, tm):
    t, n = x.shape
    k = w.shape[0]
    seq_tiles = tiles_per_batch_tm
    o_spec = pl.BlockSpec((tm, k), lambda i, j: (i, 0))
    l_spec = lambda n_hg: pl.BlockSpec((None, n_hg, tm, LANES),
                                       lambda i, j: (i // seq_tiles, 0, i % seq_tiles, 0))
    return pl.pallas_call(
        _att_out_kernel,
        grid=(t // tm, 1),
        in_specs=[o_spec, o_spec, o_spec] + [l_spec(l.shape[1]) for l in lses]
                 + [pl.BlockSpec((k, n), lambda i, j: (0, 0)),
                    pl.BlockSpec((tm, n), lambda i, j: (i, 0)),
                    _gate_spec(n, n, mod_row, 2, seq_tiles)],
        out_specs=pl.BlockSpec((tm, n), lambda i, j: (i, 0)),
        out_shape=jax.ShapeDtypeStruct((t, n), F32),
        compiler_params=_params(2),
        name="att_merge_out_proj",
    )(*[o.reshape(t, k) for o in outs], *lses, w, x, mod3)


RET_TOKENS = 1024
RET_HEADS_PER_STEP = 2


def _retention_kernel(q_ref, k_ref, v_ref, g_ref, inner_ref, cross_ref, sdec_ref, cdec_ref,
                      o_ref, state_ref):
    @pl.when(pl.program_id(2) == 0)
    def _():
        state_ref[...] = jnp.zeros_like(state_ref)

    dn_nt = (((1,), (1,)), ((), ()))
    dn_tn = (((0,), (0,)), ((), ()))
    for c in range(RET_TOKENS // RET_CHUNK):
        rows = slice(c * RET_CHUNK, (c + 1) * RET_CHUNK)
        for hh in range(RET_HEADS_PER_STEP):
            qk_cols = slice(hh * RET_HEAD_QK, (hh + 1) * RET_HEAD_QK)
            v_cols = slice(hh * RET_HEAD_V, (hh + 1) * RET_HEAD_V)
            q_b = q_ref[rows, qk_cols]
            k_b = k_ref[rows, qk_cols]
            qc_b = (q_b.astype(F32) * cross_ref[hh]).astype(BF16)
            ks_b = (k_b.astype(F32) * sdec_ref[hh]).astype(BF16)
            v = v_ref[rows, v_cols]

            scores = (lax.dot_general(q_b, k_b, dn_nt, preferred_element_type=F32)
                      * inner_ref[hh])
            state = state_ref[hh]
            o = (jnp.dot(scores.astype(BF16), v, preferred_element_type=F32)
                 + jnp.dot(qc_b, state.astype(BF16), preferred_element_type=F32))
            state_ref[hh] = state * cdec_ref[hh] + lax.dot_general(
                ks_b, v, dn_tn, preferred_element_type=F32)

            ms = jnp.mean(o * o, axis=-1, keepdims=True)
            gate = g_ref[rows, v_cols].astype(F32)
            o_ref[rows, v_cols] = (gate * (o * lax.rsqrt(ms + NORM_EPS))).astype(BF16)


def _rotary_tables(seq):
    half = RET_HEAD_QK // 2
    inv = 1.0 / (10000.0 ** jnp.linspace(0.0, 1.0, half, dtype=F32))
    ang = jnp.arange(seq, dtype=F32)[:, None] * inv[None, :]
    return jnp.cos(ang), jnp.sin(ang)


def _retention(qkvg, batch, seq):
    c = RET_CHUNK
    hq, hv = RET_HEAD_QK, RET_HEAD_V
    hps = RET_HEADS_PER_STEP
    n_qk = RET_HEADS * hq
    n_v = RET_HEADS * hv

    log_gamma = jnp.log(1.0 - 2.0 ** (-5.0 - jnp.arange(RET_HEADS, dtype=F32)))
    pos = jnp.arange(c, dtype=F32)
    diff = pos[:, None] - pos[None, :]
    inner = jnp.where(diff[None] >= 0,
                      jnp.exp(jnp.maximum(diff, 0.0)[None] * log_gamma[:, None, None]), 0.0)
    cross = jnp.exp((pos[None, :] + 1.0) * log_gamma[:, None])
    sdec = jnp.exp((c - 1.0 - pos)[None, :] * log_gamma[:, None])
    cdec = jnp.exp(c * log_gamma)
    k_scale = hq ** -0.5
    inner = inner * k_scale
    cross_l = jnp.broadcast_to(cross[:, :, None], (RET_HEADS, c, hq))
    sdec_l = jnp.broadcast_to((sdec * k_scale)[:, :, None], (RET_HEADS, c, hq))
    cdec_l = jnp.broadcast_to(cdec[:, None, None], (RET_HEADS, 1, hv))

    x = qkvg.reshape(batch, seq, qkvg.shape[1])
    tt = RET_TOKENS
    wq, wv = hps * hq, hps * hv
    head_tab = lambda shape: pl.BlockSpec((hps,) + shape, lambda b, h, s: (h, 0, 0))
    out = pl.pallas_call(
        _retention_kernel,
        grid=(batch, RET_HEADS // hps, seq // tt),
        in_specs=[pl.BlockSpec((None, tt, wq), lambda b, h, s: (b, s, h)),
                  pl.BlockSpec((None, tt, wq), lambda b, h, s: (b, s, n_qk // wq + h)),
                  pl.BlockSpec((None, tt, wv), lambda b, h, s: (b, s, 2 * n_qk // wv + h)),
                  pl.BlockSpec((None, tt, wv), lambda b, h, s: (b, s, (2 * n_qk + n_v) // wv + h)),
                  head_tab((c, c)), head_tab((c, hq)), head_tab((c, hq)), head_tab((1, hv))],
        out_specs=pl.BlockSpec((None, tt, wv), lambda b, h, s: (b, s, h)),
        out_shape=jax.ShapeDtypeStruct((batch, seq, n_v), BF16),
        scratch_shapes=[pltpu.VMEM((hps, hq, hv), F32)],
        compiler_params=_params(3),
        name="retention",
    )(x, x, x, x, inner, cross_l, sdec_l, cdec_l)
    return out.reshape(batch * seq, n_v)


def kernel(x, c, w_mod, b_mod, norm_mix, norm_ffn, rel_bias, att_w_qkv, att_q_gain, att_k_gain,
           att_w_o, ret_w_qkvg, ret_w_o, ffn_w_up, ffn_w_down):
    batch, seq, d = x.shape
    depth = w_mod.shape[0]
    assert seq % (2 * TM) == 0 and d == ATT_HEADS * HEAD_DIM
    tiles_per_batch = seq // TM

    mod = _modulation(c, w_mod, b_mod)
    mod3 = mod.reshape(depth * batch, 1, 6 * d)
    bias = _bias_tiles(rel_bias)
    xf = x.reshape(batch * seq, d)

    ready = {}

    def bf16_weight(name, layer, w):
        return ready.pop((name, layer)) if (name, layer) in ready else w.astype(BF16)

    def first_weight(layer):
        if layer >= depth:
            return None
        return (("att_w_qkv", (att_w_qkv, layer // 2)) if layer % 2 == 0
                else ("ret_w_qkvg", (ret_w_qkvg, layer // 2)))

    for i in range(depth):
        jdx = i // 2
        mod_row = i * batch
        gain_mix = norm_mix[i].reshape(1, d)
        if i % 2 == 0:
            scale = HEAD_DIM ** -0.5 * LOG2E
            ones = jnp.ones((ATT_HEADS * HEAD_DIM,), F32)
            head_gain = jnp.concatenate(
                [jnp.concatenate([jnp.tile(att_q_gain[jdx, g] * scale, ATT_HEADS),
                                  jnp.tile(att_k_gain[jdx, g], ATT_HEADS), ones])
                 for g in range(N_GROUPS)]).reshape(1, -1)
            qkv, (w_o, w_up) = _att_proj(xf, gain_mix, mod3, mod_row,
                                         bf16_weight("att_w_qkv", i, att_w_qkv[jdx]), head_gain,
                                         tiles_per_batch, [(att_w_o, jdx), (ffn_w_up, i)])
            ready[("ffn_w_up", i)] = w_up
            outs, lses = zip(*[_attention_group(qkv, bias, g, batch, seq)
                               for g in range(N_GROUPS)])
            att_tm = 512
            xf = _att_out_proj(outs, lses, w_o, xf, mod3, mod_row, seq // att_tm, att_tm)
        else:
            n_qk = RET_HEADS * RET_HEAD_QK
            qkvg, (w_o, w_up) = _proj(xf, gain_mix, mod3, mod_row,
                                      bf16_weight("ret_w_qkvg", i, ret_w_qkvg[jdx]), seq,
                                      *_rotary_tables(seq), 2 * n_qk,
                                      2 * n_qk + RET_HEADS * RET_HEAD_V,
                                      [(ret_w_o, jdx), (ffn_w_up, i)])
            ready[("ffn_w_up", i)] = w_up
            mixed = _retention(qkvg, batch, seq)
            xf, _ = _out_proj(mixed, w_o, xf, mod3, mod_row, 2, tiles_per_batch, TM, TN,
                              "ret_out_proj")

        act, (w_down,) = _ffn_up(xf, norm_ffn[i].reshape(1, d), mod3, mod_row,
                                 bf16_weight("ffn_w_up", i, ffn_w_up[i]), seq, [(ffn_w_down, i)])
        nxt = first_weight(i + 1)
        xf, cast = _out_proj(act, w_down, xf, mod3, mod_row, 5, tiles_per_batch, TM, 512,
                             "ffn_down", [nxt[1]] if nxt else [])
        if nxt:
            ready[(nxt[0], i + 1)] = cast[0]
    return xf.reshape(batch, seq, d)
```

```python
import functools
import math

import numpy as np
import jax
import jax.numpy as jnp
from jax import lax
from jax.experimental import pallas as pl
from jax.experimental.pallas import tpu as pltpu

F32 = jnp.float32
BF16 = jnp.bfloat16

NORM_EPS = 1e-6
MASK_VALUE = -1e30
LOG2E = math.log2(math.e)
LANES = 128

ATT_HEADS = 16
HEAD_DIM = 128
DILATED_PATTERNS = ((128, 1), (512, 4), (2048, 16))
N_GROUPS = len(DILATED_PATTERNS)
ATT_BLK = 128
REL_BUCKETS = 32
REL_MAX_DIST = 2048

RET_HEADS = 8
RET_HEAD_QK = 256
RET_HEAD_V = 512
RET_CHUNK = 256

VMEM_LIMIT_BYTES = 56 * 1024 * 1024

TM = 1024
TN = 1024
TN_WIDE = 2048
ROW_CHUNK = 256
MM_ROWS = 256
STAGE_COLS = 4
CAST_ROWS = 16


def _params(n_axes):
    return pltpu.CompilerParams(dimension_semantics=("arbitrary",) * n_axes,
                                vmem_limit_bytes=VMEM_LIMIT_BYTES)


def _silu(v):
    return v / (1.0 + jnp.exp(-v))


def _row_chunk(i, size):
    if isinstance(i, int):
        return pl.ds(i * size, size)
    return pl.ds(pl.multiple_of(i * size, size), size)


def _mod_kernel(c_ref, w_ref, b_ref, o_ref):
    cond = _silu(c_ref[...]).astype(BF16)
    o_ref[...] = (jnp.dot(cond, w_ref[...].astype(BF16), preferred_element_type=F32)
                  + b_ref[...])


def _modulation(c, w_mod, b_mod):
    depth, d, n = w_mod.shape
    b = c.shape[0]
    tn = 1024
    return pl.pallas_call(
        _mod_kernel,
        grid=(depth, n // tn),
        in_specs=[pl.BlockSpec((b, d), lambda l, j: (0, 0)),
                  pl.BlockSpec((None, d, tn), lambda l, j: (l, 0, j)),
                  pl.BlockSpec((None, 1, tn), lambda l, j: (l, 0, j))],
        out_specs=pl.BlockSpec((None, b, tn), lambda l, j: (l, 0, j)),
        out_shape=jax.ShapeDtypeStruct((depth, b, n), F32),
        compiler_params=_params(2),
        name="adaln_mod",
    )(c, w_mod, b_mod.reshape(depth, 1, n))


def _row_rsqrt_to_scratch(x_ref, rs_ref):
    def body(i, carry):
        rows = _row_chunk(i, ROW_CHUNK)
        xx = x_ref[rows, :]
        ms = jnp.mean(xx * xx, axis=-1, keepdims=True)
        rs_ref[rows, :] = jnp.broadcast_to(lax.rsqrt(ms + NORM_EPS), (ROW_CHUNK, LANES))
        return carry

    lax.fori_loop(0, x_ref.shape[0] // ROW_CHUNK, body, 0)


def _norm_mod_to_scratch(x_ref, gain_ref, sc_ref, sh_ref, h_ref, rs_ref):
    _row_rsqrt_to_scratch(x_ref, rs_ref)

    def body(i, carry):
        rows = _row_chunk(i, ROW_CHUNK)
        rs = rs_ref[rows, :]
        for cb in range(x_ref.shape[1] // LANES):
            cols = slice(cb * LANES, (cb + 1) * LANES)
            hn = ((x_ref[rows, cols] * rs * gain_ref[:, cols]) * (1.0 + sc_ref[:, cols])
                  + sh_ref[:, cols])
            h_ref[rows, cols] = hn.astype(BF16)
        return carry

    lax.fori_loop(0, x_ref.shape[0] // ROW_CHUNK, body, 0)


def _side_cast_specs(side, n_row_steps, n_cols_steps):
    in_specs, out_specs, shapes = [], [], []
    for w, layer in side:
        _, rows, cols = w.shape
        rb = CAST_ROWS
        while rows // rb > n_row_steps * n_cols_steps:
            rb *= 2
        assert rows % rb == 0
        block = lambda i, j, last=rows // rb - 1: jnp.minimum(i * n_cols_steps + j, last)
        in_specs.append(pl.BlockSpec((None, rb, cols),
                                     lambda i, j, b=block, l=layer: (l, b(i, j), 0)))
        out_specs.append(pl.BlockSpec((rb, cols), lambda i, j, b=block: (b(i, j), 0)))
        shapes.append(jax.ShapeDtypeStruct((rows, cols), BF16))
    return in_specs, out_specs, shapes


def _side_cast(src_refs, dst_refs):
    for src, dst in zip(src_refs, dst_refs):
        dst[...] = src[...].astype(BF16)


def _x_tile_spec(tm, d):
    return pl.BlockSpec((tm, d), lambda i, j: (i, 0), pipeline_mode=pl.Buffered(1))


def _mod_spec(d, tiles_per_batch, row, chunk):
    return pl.BlockSpec((None, 1, d), lambda i, j: (row + i // tiles_per_batch, 0, chunk))


def _att_prologue(x_ref, gain_ref, sc_ref, sh_ref, h_ref, rs_ref, stage_ref, stage2_ref):
    tm, d = x_ref.shape
    n_chunks = tm // ROW_CHUNK

    _row_rsqrt_to_scratch(x_ref, rs_ref)

    for cg in range(d // (STAGE_COLS * LANES)):
        col = lambda cc: slice((cg * STAGE_COLS + cc) * LANES, (cg * STAGE_COLS + cc + 1) * LANES)

        def natural(i, carry):
            rows = _row_chunk(i, ROW_CHUNK)
            rs = rs_ref[rows, :]
            for cc in range(STAGE_COLS):
                cols = col(cc)
                hn = ((x_ref[rows, cols] * rs * gain_ref[:, cols]) * (1.0 + sc_ref[:, cols])
                      + sh_ref[:, cols])
                h_ref[0, rows, cols] = hn.astype(BF16)
                stage_ref[cc, rows, :] = hn
            return carry

        lax.fori_loop(0, n_chunks, natural, 0)

        dil = DILATED_PATTERNS[1][1]
        per = tm // dil
        sub = DILATED_PATTERNS[2][1] // dil
        per16 = per // sub

        def permute4(r, carry):
            dst = _row_chunk(r, per)
            for cc in range(STAGE_COLS):
                v = stage_ref[cc, pl.ds(r, per, stride=dil), :]
                h_ref[1, dst, col(cc)] = v.astype(BF16)
                stage2_ref[cc, dst, :] = v
            return carry

        lax.fori_loop(0, dil, permute4, 0)

        def permute16(r, carry):
            src = pl.ds((r % dil) * per + r // dil, per16, stride=sub)
            dst = _row_chunk(r, per16)
            for cc in range(STAGE_COLS):
                h_ref[2, dst, col(cc)] = stage2_ref[cc, src, :].astype(BF16)
            return carry

        lax.fori_loop(0, dil * sub, permute16, 0)


def _att_proj_kernel(x_ref, gain_ref, sc_ref, sh_ref, w_ref, hg_ref, *rest, d_model, n_side):
    side_in, rest = rest[:n_side], rest[n_side:]
    o_ref, side_out = rest[0], rest[1:1 + n_side]
    h_ref, rs_ref, stage_ref, stage2_ref = rest[1 + n_side:]
    j = pl.program_id(1)

    @pl.when(j == 0)
    def _():
        _att_prologue(x_ref, gain_ref, sc_ref, sh_ref, h_ref, rs_ref, stage_ref, stage2_ref)

    tn = w_ref.shape[1]
    group = (j * tn) // (3 * d_model)
    is_v = ((j * tn) // d_model) % 3 == 2

    def tile(head_norm):
        _side_cast(side_in, side_out)
        for rc in range(h_ref.shape[1] // MM_ROWS):
            rows = slice(rc * MM_ROWS, (rc + 1) * MM_ROWS)
            acc = jnp.dot(h_ref[group, rows, :], w_ref[...], preferred_element_type=F32)
            if not head_norm:
                o_ref[rows, :] = acc.astype(BF16)
                continue
            for hh in range(tn // HEAD_DIM):
                sl = slice(hh * HEAD_DIM, (hh + 1) * HEAD_DIM)
                a = acc[:, sl]
                ms = jnp.mean(a * a, axis=-1, keepdims=True)
                o_ref[rows, sl] = (a * lax.rsqrt(ms + NORM_EPS) * hg_ref[:, sl]).astype(BF16)

    pl.when(is_v)(functools.partial(tile, False))
    pl.when(jnp.logical_not(is_v))(functools.partial(tile, True))


def _att_proj(x, gain, mod3, mod_row, w, head_gain, tiles_per_batch, side):
    t, d = x.shape
    n = w.shape[1]
    side_in, side_out, side_shapes = _side_cast_specs(side, t // TM, n // TN_WIDE)
    outs = pl.pallas_call(
        functools.partial(_att_proj_kernel, d_model=d, n_side=len(side)),
        grid=(t // TM, n // TN_WIDE),
        in_specs=[_x_tile_spec(TM, d),
                  pl.BlockSpec((1, d), lambda i, j: (0, 0)),
                  _mod_spec(d, tiles_per_batch, mod_row, 1),
                  _mod_spec(d, tiles_per_batch, mod_row, 0),
                  pl.BlockSpec((d, TN_WIDE), lambda i, j: (0, j)),
                  pl.BlockSpec((1, TN_WIDE), lambda i, j: (0, j))] + side_in,
        out_specs=[pl.BlockSpec((TM, TN_WIDE), lambda i, j: (i, j))] + side_out,
        out_shape=[jax.ShapeDtypeStruct((t, n), BF16)] + side_shapes,
        scratch_shapes=[pltpu.VMEM((N_GROUPS, TM, d), BF16),
                        pltpu.VMEM((TM, LANES), F32),
                        pltpu.VMEM((STAGE_COLS, TM, LANES), F32),
                        pltpu.VMEM((STAGE_COLS, TM, LANES), F32)],
        compiler_params=_params(2),
        name="att_qkv_proj",
    )(x, gain, mod3, mod3, w, head_gain, *[w for w, _ in side])
    return outs[0], outs[1:]


def _proj_kernel(x_ref, gain_ref, sc_ref, sh_ref, cos_ref, sin_ref, w_ref, *rest, rot_cols,
                 gate_col0, n_side):
    side_in, rest = rest[:n_side], rest[n_side:]
    o_ref, side_out = rest[0], rest[1:1 + n_side]
    h_ref, rs_ref = rest[1 + n_side:]
    j = pl.program_id(1)

    @pl.when(j == 0)
    def _():
        _norm_mod_to_scratch(x_ref, gain_ref, sc_ref, sh_ref, h_ref, rs_ref)

    tn = w_ref.shape[1]
    half = RET_HEAD_QK // 2

    def tile(kind):
        _side_cast(side_in, side_out)
        for rc in range(h_ref.shape[0] // MM_ROWS):
            rows = slice(rc * MM_ROWS, (rc + 1) * MM_ROWS)
            acc = jnp.dot(h_ref[rows, :], w_ref[...], preferred_element_type=F32)
            if kind == "rotary":
                cos, sin = cos_ref[rows, :], sin_ref[rows, :]
                for hh in range(tn // RET_HEAD_QK):
                    lo = slice(hh * RET_HEAD_QK, hh * RET_HEAD_QK + half)
                    hi = slice(hh * RET_HEAD_QK + half, (hh + 1) * RET_HEAD_QK)
                    o_ref[rows, lo] = (acc[:, lo] * cos - acc[:, hi] * sin).astype(BF16)
                    o_ref[rows, hi] = (acc[:, lo] * sin + acc[:, hi] * cos).astype(BF16)
            else:
                o_ref[rows, :] = (_silu(acc) if kind == "silu" else acc).astype(BF16)

    col0 = j * tn
    pl.when(col0 < rot_cols)(functools.partial(tile, "rotary"))
    pl.when(jnp.logical_and(col0 >= rot_cols, col0 < gate_col0))(functools.partial(tile, "plain"))
    pl.when(col0 >= gate_col0)(functools.partial(tile, "silu"))


def _proj(x, gain, mod3, mod_row, w, seq, cos, sin, rot_cols, gate_col0, side):
    t, d = x.shape
    n = w.shape[1]
    tm = TM
    tiles_per_batch = seq // tm
    tn = TN_WIDE
    assert rot_cols % tn == 0 and gate_col0 % tn == 0
    side_in, side_out, side_shapes = _side_cast_specs(side, t // tm, n // tn)
    pos_spec = pl.BlockSpec((tm, cos.shape[1]), lambda i, j: (i % tiles_per_batch, 0))
    outs = pl.pallas_call(
        functools.partial(_proj_kernel, rot_cols=rot_cols, gate_col0=gate_col0,
                          n_side=len(side)),
        grid=(t // tm, n // tn),
        in_specs=[pl.BlockSpec((tm, d), lambda i, j: (i, 0)),
                  pl.BlockSpec((1, d), lambda i, j: (0, 0)),
                  _mod_spec(d, tiles_per_batch, mod_row, 1),
                  _mod_spec(d, tiles_per_batch, mod_row, 0),
                  pos_spec, pos_spec,
                  pl.BlockSpec((d, tn), lambda i, j: (0, j))] + side_in,
        out_specs=[pl.BlockSpec((tm, tn), lambda i, j: (i, j))] + side_out,
        out_shape=[jax.ShapeDtypeStruct((t, n), BF16)] + side_shapes,
        scratch_shapes=[pltpu.VMEM((tm, d), BF16), pltpu.VMEM((tm, LANES), F32)],
        compiler_params=_params(2),
        name="ret_qkvg_proj",
    )(x, gain, mod3, mod3, cos, sin, w, *[w for w, _ in side])
    return outs[0], outs[1:]


def _ffn_up_kernel(x_ref, gain_ref, sc_ref, sh_ref, wg_ref, wu_ref, *rest, n_side):
    side_in, rest = rest[:n_side], rest[n_side:]
    o_ref, side_out = rest[0], rest[1:1 + n_side]
    h_ref, rs_ref = rest[1 + n_side:]

    @pl.when(pl.program_id(1) == 0)
    def _():
        _norm_mod_to_scratch(x_ref, gain_ref, sc_ref, sh_ref, h_ref, rs_ref)

    _side_cast(side_in, side_out)
    for rc in range(h_ref.shape[0] // MM_ROWS):
        rows = slice(rc * MM_ROWS, (rc + 1) * MM_ROWS)
        h = h_ref[rows, :]
        gate = jnp.dot(h, wg_ref[...], preferred_element_type=F32)
        up = jnp.dot(h, wu_ref[...], preferred_element_type=F32)
        o_ref[rows, :] = (_silu(gate) * up).astype(BF16)


def _ffn_up(x, gain, mod3, mod_row, w_up, seq, side):
    t, d = x.shape
    d_ff = w_up.shape[1] // 2
    tf = 512
    n_f = d_ff // tf
    tm = TM
    tiles_per_batch = seq // tm
    side_in, side_out, side_shapes = _side_cast_specs(side, t // tm, n_f)
    outs = pl.pallas_call(
        functools.partial(_ffn_up_kernel, n_side=len(side)),
        grid=(t // tm, n_f),
        in_specs=[pl.BlockSpec((tm, d), lambda i, j: (i, 0)),
                  pl.BlockSpec((1, d), lambda i, j: (0, 0)),
                  _mod_spec(d, tiles_per_batch, mod_row, 4),
                  _mod_spec(d, tiles_per_batch, mod_row, 3),
                  pl.BlockSpec((d, tf), lambda i, j: (0, j)),
                  pl.BlockSpec((d, tf), lambda i, j: (0, j + n_f))] + side_in,
        out_specs=[pl.BlockSpec((tm, tf), lambda i, j: (i, j))] + side_out,
        out_shape=[jax.ShapeDtypeStruct((t, d_ff), BF16)] + side_shapes,
        scratch_shapes=[pltpu.VMEM((tm, d), BF16), pltpu.VMEM((tm, LANES), F32)],
        compiler_params=_params(2),
        name="ffn_up",
    )(x, gain, mod3, mod3, w_up, w_up, *[w for w, _ in side])
    return outs[0], outs[1:]


def _out_proj_kernel(a_ref, w_ref, x_ref, g_ref, *rest, n_side):
    side_in, o_ref, side_out = rest[:n_side], rest[n_side], rest[n_side + 1:]
    _side_cast(side_in, side_out)
    for rc in range(a_ref.shape[0] // MM_ROWS):
        rows = slice(rc * MM_ROWS, (rc + 1) * MM_ROWS)
        y = jnp.dot(a_ref[rows, :], w_ref[...], preferred_element_type=F32)
        o_ref[rows, :] = x_ref[rows, :] + g_ref[...] * y


def _gate_spec(n, tn, mod_row, gate_chunk, tiles_per_batch):
    return pl.BlockSpec((None, 1, tn),
                        lambda i, j: (mod_row + i // tiles_per_batch, 0,
                                      gate_chunk * (n // tn) + j))


def _out_proj(a, w, x, mod3, mod_row, gate_chunk, tiles_per_batch, tm, tn, name, side=()):
    t, k = a.shape
    n = w.shape[1]
    side_in, side_out, side_shapes = _side_cast_specs(side, t // tm, n // tn)
    outs = pl.pallas_call(
        functools.partial(_out_proj_kernel, n_side=len(side)),
        grid=(t // tm, n // tn),
        in_specs=[pl.BlockSpec((tm, k), lambda i, j: (i, 0)),
                  pl.BlockSpec((k, tn), lambda i, j: (0, j)),
                  pl.BlockSpec((tm, tn), lambda i, j: (i, j)),
                  _gate_spec(n, tn, mod_row, gate_chunk, tiles_per_batch)] + side_in,
        out_specs=[pl.BlockSpec((tm, tn), lambda i, j: (i, j))] + side_out,
        out_shape=[jax.ShapeDtypeStruct((t, n), F32)] + side_shapes,
        compiler_params=_params(2),
        name=name,
    )(a, w, x, mod3, *[w for w, _ in side])
    return outs[0], outs[1:]


def _bias_kernel(tab_ref, bucket_ref, valid_ref, o_ref):
    g = pl.program_id(0)
    h = pl.program_id(1)
    bucket = bucket_ref[...]
    col = g * ATT_HEADS + h
    acc = jnp.zeros(bucket.shape, F32)
    for b in range(REL_BUCKETS):
        acc = jnp.where(bucket == b, tab_ref[b, col] * LOG2E, acc)
    for first in range(2):
        o_ref[first] = jnp.where(valid_ref[first] != 0, acc, MASK_VALUE)


def _t5_bucket_np(dist):
    max_exact = REL_BUCKETS // 2
    d_f = np.maximum(dist, 1).astype(np.float32)
    large = max_exact + (np.log(d_f / np.float32(max_exact))
                         / np.float32(math.log(REL_MAX_DIST / max_exact))
                         * np.float32(REL_BUCKETS - max_exact)).astype(np.int32)
    large = np.minimum(large, REL_BUCKETS - 1)
    return np.where(dist < max_exact, dist, large).astype(np.int32)


def _bias_tiles(rel_bias):
    blk = ATT_BLK
    qi = np.arange(blk)[:, None]
    ki = np.arange(2 * blk)[None, :]
    delta = blk + qi - ki
    band = (delta >= 0) & (delta <= blk)
    valid = np.stack([band & (ki >= blk), band]).astype(np.int32)
    buckets = np.stack([_t5_bucket_np(np.maximum(delta, 0) * dil)
                        for _, dil in DILATED_PATTERNS])
    return pl.pallas_call(
        _bias_kernel,
        grid=(N_GROUPS, ATT_HEADS),
        in_specs=[pl.BlockSpec(memory_space=pltpu.SMEM),
                  pl.BlockSpec((None, blk, 2 * blk), lambda g, h: (g, 0, 0)),
                  pl.BlockSpec((2, blk, 2 * blk), lambda g, h: (0, 0, 0))],
        out_specs=pl.BlockSpec((None, 2, None, blk, 2 * blk), lambda g, h: (g, 0, h, 0, 0)),
        out_shape=jax.ShapeDtypeStruct((N_GROUPS, 2, ATT_HEADS, blk, 2 * blk), F32),
        compiler_params=_params(2),
        name="rel_bias_tiles",
    )(rel_bias, jnp.asarray(buckets), jnp.asarray(valid))


ATT_PASS = 8


def _att_units(q_ref, kc_ref, kp_ref, vc_ref, vp_ref, bias_ref, store, *, g, heads, first_span):
    blk = ATT_BLK
    dil = DILATED_PATTERNS[g][1]
    dn = (((1,), (1,)), ((), ()))
    ones = jnp.ones((2 * blk, HEAD_DIM), BF16)
    lane = lax.broadcasted_iota(jnp.int32, (blk, LANES), 1)
    head0 = pl.program_id(1) * heads

    def group(units):
        sl = lambda h: slice(h * HEAD_DIM, (h + 1) * HEAD_DIM)
        pairs = [(ui, h) for ui in range(len(units)) for h in range(heads)]
        m_tiles = [jnp.zeros((blk, LANES), F32) for _ in units]
        l_tiles = [jnp.ones((blk, LANES), F32) for _ in units]
        for p0 in range(0, len(pairs), ATT_PASS):
            batch = pairs[p0:p0 + ATT_PASS]
            pieces = []
            for ui, h in batch:
                _, q_of, k_of, _, variant = units[ui]
                pieces.append(lax.dot_general(q_of(sl(h)), k_of(sl(h)), dn,
                                              preferred_element_type=F32)
                              + bias_ref[variant, h])
            s = jnp.concatenate(pieces, axis=0)
            m = jnp.max(jnp.maximum(s[:, :blk], s[:, blk:]), axis=-1, keepdims=True)
            p = jnp.exp2(s - m).astype(BF16)
            for idx, (ui, h) in enumerate(batch):
                u, _, _, v_of, _ = units[ui]
                rows = slice(idx * blk, (idx + 1) * blk)
                v_ext = jnp.concatenate([v_of(sl(h)), ones], axis=1)
                pv = jnp.dot(p[rows], v_ext, preferred_element_type=F32)
                denom = pv[:, HEAD_DIM:]
                o = pv[:, :HEAD_DIM] / denom
                mine = lane == head0 + h
                m_tiles[ui] = jnp.where(mine, m[rows], m_tiles[ui])
                l_tiles[ui] = jnp.where(mine, denom, l_tiles[ui])
                store(u, h, o, None)
        for ui, unit_args in enumerate(units):
            store(unit_args[0], None, None, m_tiles[ui] + jnp.log2(l_tiles[ui]))

    unit = lambda *unit_args: group([unit_args])

    if dil == 1:
        n_units = q_ref.shape[0] // blk
        unit(0,
             lambda sl: q_ref[0:blk, sl],
             lambda sl: jnp.concatenate([kp_ref[:, sl], kc_ref[0:blk, sl]], axis=0),
             lambda sl: jnp.concatenate([vp_ref[:, sl], vc_ref[0:blk, sl]], axis=0),
             jnp.where(first_span, 0, 1))

        def body(u, carry):
            cur = _row_chunk(u, blk)
            both = pl.ds(pl.multiple_of((u - 1) * blk, blk), 2 * blk)
            unit(u, lambda sl: q_ref[cur, sl], lambda sl: kc_ref[both, sl],
                 lambda sl: vc_ref[both, sl], 1)
            return carry

        lax.fori_loop(1, n_units, body, 0)
    else:
        variant = jnp.where(first_span, 0, 1)
        if dil == 4:
            pick = lambda ref, r, sl: ref[r, :, sl]
            prev_cur = lambda pref, cref, r, sl: jnp.concatenate(
                [pref[r, :, sl], cref[r, :, sl]], axis=0)
        else:
            pick = lambda ref, r, sl: jnp.concatenate([ref[0, r, :, sl], ref[1, r, :, sl]], axis=0)
            prev_cur = lambda pref, cref, r, sl: jnp.concatenate(
                [pref[0, r, :, sl], pref[1, r, :, sl], cref[0, r, :, sl], cref[1, r, :, sl]],
                axis=0)

        per_iter = ATT_HEADS // heads

        def body(it, carry):
            def unit_args(r):
                return (r, lambda sl: pick(q_ref, r, sl),
                        lambda sl: prev_cur(kp_ref, kc_ref, r, sl),
                        lambda sl: prev_cur(vp_ref, vc_ref, r, sl), variant)

            group([unit_args(it * per_iter + sub) for sub in range(per_iter)])
            return carry

        lax.fori_loop(0, dil // per_iter, body, 0)


def _att_kernel(q_ref, kc_ref, vc_ref, bias_ref, o_ref, lse_ref, kp_ref, vp_ref, *scratch,
                g, heads):
    blk = ATT_BLK
    dil = DILATED_PATTERNS[g][1]
    first_span = pl.program_id(2) == 0

    @pl.when(first_span)
    def _():
        kp_ref[...] = jnp.zeros_like(kp_ref)
        vp_ref[...] = jnp.zeros_like(vp_ref)

    if dil == 1:
        def store(u, h, o, lse_tile):
            rows = _row_chunk(u, blk)
            if h is None:
                lse_ref[rows, :] = lse_tile
            else:
                o_ref[rows, h * HEAD_DIM:(h + 1) * HEAD_DIM] = o.astype(BF16)
    else:
        o_scr, lse_scr = scratch

        def store(u, h, o, lse_tile):
            rows = pl.ds(u, blk, stride=dil)
            if h is None:
                lse_scr[rows, :] = lse_tile
            else:
                o_scr[h, rows, :] = o

    _att_units(q_ref, kc_ref, kp_ref, vc_ref, vp_ref, bias_ref, store, g=g, heads=heads,
               first_span=first_span)

    if dil == 1:
        last = slice(kc_ref.shape[0] - blk, kc_ref.shape[0])
        kp_ref[...] = kc_ref[last, :]
        vp_ref[...] = vc_ref[last, :]
    else:
        kp_ref[...] = kc_ref[...]
        vp_ref[...] = vc_ref[...]
        for h in range(heads):
            o_ref[:, h * HEAD_DIM:(h + 1) * HEAD_DIM] = o_scr[h].astype(BF16)
        lse_ref[...] = lse_scr[...]


def _attention_group(qkv, bias, g, batch, seq):
    dil = DILATED_PATTERNS[g][1]
    blk = ATT_BLK
    span = dil * blk if dil > 1 else 8 * blk
    heads = ATT_HEADS if dil < 16 else 4
    n_hg = ATT_HEADS // heads
    width = heads * HEAD_DIM
    n_cols = qkv.shape[1]
    col0 = lambda which: (g * 3 + which) * (ATT_HEADS // heads)

    if dil == 1:
        view = qkv.reshape(batch, seq, n_cols)
        cur = lambda which: pl.BlockSpec((None, span, width),
                                         lambda b, h, s: (b, s, col0(which) + h))
        prev_shape = (blk, width)
        scratch = []
    elif dil == 4:
        halves = TM // (dil * blk)
        view = qkv.reshape(batch, seq // TM, dil, halves, blk, n_cols)
        cur = lambda which: pl.BlockSpec(
            (None, None, dil, None, blk, width),
            lambda b, h, s: (b, s // halves, 0, s % halves, 0, col0(which) + h))
        prev_shape = (dil, blk, width)
        scratch = [pltpu.VMEM((heads, span, HEAD_DIM), F32), pltpu.VMEM((span, LANES), F32)]
    else:
        tiles = span // TM
        per = TM // dil
        view = qkv.reshape(batch, seq // span, tiles, dil, per, n_cols)
        cur = lambda which: pl.BlockSpec((None, None, tiles, dil, per, width),
                                         lambda b, h, s: (b, s, 0, 0, 0, col0(which) + h))
        prev_shape = (tiles, dil, per, width)
        scratch = [pltpu.VMEM((heads, span, HEAD_DIM), F32), pltpu.VMEM((span, LANES), F32)]

    o, lse = pl.pallas_call(
        functools.partial(_att_kernel, g=g, heads=heads),
        grid=(batch, n_hg, seq // span),
        in_specs=[cur(0), cur(1), cur(2),
                  pl.BlockSpec((None, 2, heads, blk, 2 * blk), lambda b, h, s: (g, 0, h, 0, 0))],
        out_specs=[pl.BlockSpec((None, span, width), lambda b, h, s: (b, s, h)),
                   pl.BlockSpec((None, None, span, LANES), lambda b, h, s: (b, h, s, 0))],
        out_shape=[jax.ShapeDtypeStruct((batch, seq, ATT_HEADS * HEAD_DIM), BF16),
                   jax.ShapeDtypeStruct((batch, n_hg, seq, LANES), F32)],
        scratch_shapes=[pltpu.VMEM(prev_shape, BF16), pltpu.VMEM(prev_shape, BF16)] + scratch,
        compiler_params=_params(3),
        name=f"dilated_attention_g{g}",
    )(view, view, view, bias)
    return o, lse


MERGE_ROWS = 256


def _att_out_kernel(o0_ref, o1_ref, o2_ref, l0_ref, l1_ref, l2_ref, w_ref, x_ref, g_ref,
                    out_ref):
    o_refs = (o0_ref, o1_ref, o2_ref)
    l_refs = (l0_ref, l1_ref, l2_ref)
    tm = x_ref.shape[0]
    lane = lax.broadcasted_iota(jnp.int32, (MERGE_ROWS, LANES), 1)
    for rc in range(tm // MERGE_ROWS):
        rows = slice(rc * MERGE_ROWS, (rc + 1) * MERGE_ROWS)
        lses = []
        for l_ref in l_refs:
            tile = l_ref[0, rows, :]
            for hg in range(1, l_ref.shape[0]):
                tile = tile + l_ref[hg, rows, :]
            lses.append(tile)
        mx = jnp.maximum(jnp.maximum(lses[0], lses[1]), lses[2])
        es = [jnp.exp2(t - mx) for t in lses]
        den = es[0] + es[1] + es[2]
        ws = [es[0] / den, es[1] / den]
        pieces = []
        for h in range(ATT_HEADS):
            sl = slice(h * HEAD_DIM, (h + 1) * HEAD_DIM)
            base = o_refs[2][rows, sl].astype(F32)
            acc = base
            for w_g, o_ref in zip(ws, o_refs[:2]):
                w_col = jnp.sum(jnp.where(lane == h, w_g, 0.0), axis=-1, keepdims=True)
                acc = acc + w_col * (o_ref[rows, sl].astype(F32) - base)
            pieces.append(acc.astype(BF16))
        merged = jnp.concatenate(pieces, axis=1)
        y = jnp.dot(merged, w_ref[...], preferred_element_type=F32)
        out_ref[rows, :] = x_ref[rows, :] + g_ref[...] * y


def _att_out_proj(outs, lses, w, x, mod3, mod_row, tiles_per_batch_tm, tm):
    t, n = x.shape
    k = w.shape[0]
    seq_tiles = tiles_per_batch_tm
    o_spec = pl.BlockSpec((tm, k), lambda i, j: (i, 0))
    l_spec = lambda n_hg: pl.BlockSpec((None, n_hg, tm, LANES),
                                       lambda i, j: (i // seq_tiles, 0, i % seq_tiles, 0))
    return pl.pallas_call(
        _att_out_kernel,
        grid=(t // tm, 1),
        in_specs=[o_spec, o_spec, o_spec] + [l_spec(l.shape[1]) for l in lses]
                 + [pl.BlockSpec((k, n), lambda i, j: (0, 0)),
                    pl.BlockSpec((tm, n), lambda i, j: (i, 0)),
                    _gate_spec(n, n, mod_row, 2, seq_tiles)],
        out_specs=pl.BlockSpec((tm, n), lambda i, j: (i, 0)),
        out_shape=jax.ShapeDtypeStruct((t, n), F32),
        compiler_params=_params(2),
        name="att_merge_out_proj",
    )(*[o.reshape(t, k) for o in outs], *lses, w, x, mod3)


RET_TOKENS = 2048
RET_HEADS_PER_STEP = 2


def _retention_kernel(q_ref, k_ref, v_ref, g_ref, inner_ref, cross_ref, sdec_ref, cdec_ref,
                      o_ref, state_ref):
    @pl.when(pl.program_id(2) == 0)
    def _():
        state_ref[...] = jnp.zeros_like(state_ref)

    dn_nt = (((1,), (1,)), ((), ()))
    dn_tn = (((0,), (0,)), ((), ()))
    for c in range(RET_TOKENS // RET_CHUNK):
        rows = slice(c * RET_CHUNK, (c + 1) * RET_CHUNK)
        for hh in range(RET_HEADS_PER_STEP):
            qk_cols = slice(hh * RET_HEAD_QK, (hh + 1) * RET_HEAD_QK)
            v_cols = slice(hh * RET_HEAD_V, (hh + 1) * RET_HEAD_V)
            q_b = q_ref[rows, qk_cols]
            k_b = k_ref[rows, qk_cols]
            qc_b = (q_b.astype(F32) * cross_ref[hh]).astype(BF16)
            ks_b = (k_b.astype(F32) * sdec_ref[hh]).astype(BF16)
            v = v_ref[rows, v_cols]

            scores = (lax.dot_general(q_b, k_b, dn_nt, preferred_element_type=F32)
                      * inner_ref[hh])
            state = state_ref[hh]
            o = (jnp.dot(scores.astype(BF16), v, preferred_element_type=F32)
                 + jnp.dot(qc_b, state.astype(BF16), preferred_element_type=F32))
            state_ref[hh] = state * cdec_ref[hh] + lax.dot_general(
                ks_b, v, dn_tn, preferred_element_type=F32)

            ms = jnp.mean(o * o, axis=-1, keepdims=True)
            gate = g_ref[rows, v_cols].astype(F32)
            o_ref[rows, v_cols] = (gate * (o * lax.rsqrt(ms + NORM_EPS))).astype(BF16)


def _rotary_tables(seq):
    half = RET_HEAD_QK // 2
    inv = 1.0 / (10000.0 ** jnp.linspace(0.0, 1.0, half, dtype=F32))
    ang = jnp.arange(seq, dtype=F32)[:, None] * inv[None, :]
    return jnp.cos(ang), jnp.sin(ang)


def _retention(qkvg, batch, seq):
    c = RET_CHUNK
    hq, hv = RET_HEAD_QK, RET_HEAD_V
    hps = RET_HEADS_PER_STEP
    n_qk = RET_HEADS * hq
    n_v = RET_HEADS * hv

    log_gamma = jnp.log(1.0 - 2.0 ** (-5.0 - jnp.arange(RET_HEADS, dtype=F32)))
    pos = jnp.arange(c, dtype=F32)
    diff = pos[:, None] - pos[None, :]
    inner = jnp.where(diff[None] >= 0,
                      jnp.exp(jnp.maximum(diff, 0.0)[None] * log_gamma[:, None, None]), 0.0)
    cross = jnp.exp((pos[None, :] + 1.0) * log_gamma[:, None])
    sdec = jnp.exp((c - 1.0 - pos)[None, :] * log_gamma[:, None])
    cdec = jnp.exp(c * log_gamma)
    k_scale = hq ** -0.5
    inner = inner * k_scale
    cross_l = jnp.broadcast_to(cross[:, :, None], (RET_HEADS, c, hq))
    sdec_l = jnp.broadcast_to((sdec * k_scale)[:, :, None], (RET_HEADS, c, hq))
    cdec_l = jnp.broadcast_to(cdec[:, None, None], (RET_HEADS, 1, hv))

    x = qkvg.reshape(batch, seq, qkvg.shape[1])
    tt = RET_TOKENS
    wq, wv = hps * hq, hps * hv
    head_tab = lambda shape: pl.BlockSpec((hps,) + shape, lambda b, h, s: (h, 0, 0))
    out = pl.pallas_call(
        _retention_kernel,
        grid=(batch, RET_HEADS // hps, seq // tt),
        in_specs=[pl.BlockSpec((None, tt, wq), lambda b, h, s: (b, s, h)),
                  pl.BlockSpec((None, tt, wq), lambda b, h, s: (b, s, n_qk // wq + h)),
                  pl.BlockSpec((None, tt, wv), lambda b, h, s: (b, s, 2 * n_qk // wv + h)),
                  pl.BlockSpec((None, tt, wv), lambda b, h, s: (b, s, (2 * n_qk + n_v) // wv + h)),
                  head_tab((c, c)), head_tab((c, hq)), head_tab((c, hq)), head_tab((1, hv))],
        out_specs=pl.BlockSpec((None, tt, wv), lambda b, h, s: (b, s, h)),
        out_shape=jax.ShapeDtypeStruct((batch, seq, n_v), BF16),
        scratch_shapes=[pltpu.VMEM((hps, hq, hv), F32)],
        compiler_params=_params(3),
        name="retention",
    )(x, x, x, x, inner, cross_l, sdec_l, cdec_l)
    return out.reshape(batch * seq, n_v)


def kernel(x, c, w_mod, b_mod, norm_mix, norm_ffn, rel_bias, att_w_qkv, att_q_gain, att_k_gain,
           att_w_o, ret_w_qkvg, ret_w_o, ffn_w_up, ffn_w_down):
    batch, seq, d = x.shape
    depth = w_mod.shape[0]
    assert seq % (2 * TM) == 0 and d == ATT_HEADS * HEAD_DIM
    tiles_per_batch = seq // TM

    mod = _modulation(c, w_mod, b_mod)
    mod3 = mod.reshape(depth * batch, 1, 6 * d)
    bias = _bias_tiles(rel_bias)
    xf = x.reshape(batch * seq, d)

    ready = {}

    def bf16_weight(name, layer, w):
        return ready.pop((name, layer)) if (name, layer) in ready else w.astype(BF16)

    def first_weight(layer):
        if layer >= depth:
            return None
        return (("att_w_qkv", (att_w_qkv, layer // 2)) if layer % 2 == 0
                else ("ret_w_qkvg", (ret_w_qkvg, layer // 2)))

    for i in range(depth):
        jdx = i // 2
        mod_row = i * batch
        gain_mix = norm_mix[i].reshape(1, d)
        if i % 2 == 0:
            scale = HEAD_DIM ** -0.5 * LOG2E
            ones = jnp.ones((ATT_HEADS * HEAD_DIM,), F32)
            head_gain = jnp.concatenate(
                [jnp.concatenate([jnp.tile(att_q_gain[jdx, g] * scale, ATT_HEADS),
                                  jnp.tile(att_k_gain[jdx, g], ATT_HEADS), ones])
                 for g in range(N_GROUPS)]).reshape(1, -1)
            qkv, (w_o, w_up) = _att_proj(xf, gain_mix, mod3, mod_row,
                                         bf16_weight("att_w_qkv", i, att_w_qkv[jdx]), head_gain,
                                         tiles_per_batch, [(att_w_o, jdx), (ffn_w_up, i)])
            ready[("ffn_w_up", i)] = w_up
            outs, lses = zip(*[_attention_group(qkv, bias, g, batch, seq)
                               for g in range(N_GROUPS)])
            att_tm = 512
            xf = _att_out_proj(outs, lses, w_o, xf, mod3, mod_row, seq // att_tm, att_tm)
        else:
            n_qk = RET_HEADS * RET_HEAD_QK
            qkvg, (w_o, w_up) = _proj(xf, gain_mix, mod3, mod_row,
                                      bf16_weight("ret_w_qkvg", i, ret_w_qkvg[jdx]), seq,
                                      *_rotary_tables(seq), 2 * n_qk,
                                      2 * n_qk + RET_HEADS * RET_HEAD_V,
                                      [(ret_w_o, jdx), (ffn_w_up, i)])
            ready[("ffn_w_up", i)] = w_up
            mixed = _retention(qkvg, batch, seq)
            xf, _ = _out_proj(mixed, w_o, xf, mod3, mod_row, 2, tiles_per_batch, TM, TN,
                              "ret_out_proj")

        act, (w_down,) = _ffn_up(xf, norm_ffn[i].reshape(1, d), mod3, mod_row,
                                 bf16_weight("ffn_w_up", i, ffn_w_up[i]), seq, [(ffn_w_down, i)])
        nxt = first_weight(i + 1)
        xf, cast = _out_proj(act, w_down, xf, mod3, mod_row, 5, tiles_per_batch, TM, 512,
                             "ffn_down", [nxt[1]] if nxt else [])
        if nxt:
            ready[(nxt[0], i + 1)] = cast[0]
    return xf.reshape(batch, seq, d)
```

```python
import functools
import math

import numpy as np
import jax
import jax.numpy as jnp
from jax import lax
from jax.experimental import pallas as pl
from jax.experimental.pallas import tpu as pltpu

F32 = jnp.float32
BF16 = jnp.bfloat16

NORM_EPS = 1e-6
MASK_VALUE = -1e30
LOG2E = math.log2(math.e)
LANES = 128

ATT_HEADS = 16
HEAD_DIM = 128
DILATED_PATTERNS = ((128, 1), (512, 4), (2048, 16))
N_GROUPS = len(DILATED_PATTERNS)
ATT_BLK = 128
REL_BUCKETS = 32
REL_MAX_DIST = 2048

RET_HEADS = 8
RET_HEAD_QK = 256
RET_HEAD_V = 512
RET_CHUNK = 256

VMEM_LIMIT_BYTES = 56 * 1024 * 1024

TM = 1024
TN = 1024
TN_WIDE = 2048
FFN_DOWN_TM = 512
ROW_CHUNK = 256
MM_ROWS = 256
STAGE_COLS = 4
CAST_ROWS = 16


def _params(n_axes):
    return pltpu.CompilerParams(dimension_semantics=("arbitrary",) * n_axes,
                                vmem_limit_bytes=VMEM_LIMIT_BYTES)


def _silu(v):
    return v / (1.0 + jnp.exp(-v))


def _row_chunk(i, size):
    if isinstance(i, int):
        return pl.ds(i * size, size)
    return pl.ds(pl.multiple_of(i * size, size), size)


def _mod_kernel(c_ref, w_ref, b_ref, o_ref):
    cond = _silu(c_ref[...]).astype(BF16)
    o_ref[...] = (jnp.dot(cond, w_ref[...].astype(BF16), preferred_element_type=F32)
                  + b_ref[...])


def _modulation(c, w_mod, b_mod):
    depth, d, n = w_mod.shape
    b = c.shape[0]
    tn = 1024
    return pl.pallas_call(
        _mod_kernel,
        grid=(depth, n // tn),
        in_specs=[pl.BlockSpec((b, d), lambda l, j: (0, 0)),
                  pl.BlockSpec((None, d, tn), lambda l, j: (l, 0, j)),
                  pl.BlockSpec((None, 1, tn), lambda l, j: (l, 0, j))],
        out_specs=pl.BlockSpec((None, b, tn), lambda l, j: (l, 0, j)),
        out_shape=jax.ShapeDtypeStruct((depth, b, n), F32),
        compiler_params=_params(2),
        name="adaln_mod",
    )(c, w_mod, b_mod.reshape(depth, 1, n))


def _row_rsqrt_to_scratch(x_ref, rs_ref):
    def body(i, carry):
        rows = _row_chunk(i, ROW_CHUNK)
        xx = x_ref[rows, :]
        ms = jnp.mean(xx * xx, axis=-1, keepdims=True)
        rs_ref[rows, :] = jnp.broadcast_to(lax.rsqrt(ms + NORM_EPS), (ROW_CHUNK, LANES))
        return carry

    lax.fori_loop(0, x_ref.shape[0] // ROW_CHUNK, body, 0)


def _norm_mod_to_scratch(x_ref, gain_ref, sc_ref, sh_ref, h_ref, rs_ref):
    _row_rsqrt_to_scratch(x_ref, rs_ref)

    def body(i, carry):
        rows = _row_chunk(i, ROW_CHUNK)
        rs = rs_ref[rows, :]
        for cb in range(x_ref.shape[1] // LANES):
            cols = slice(cb * LANES, (cb + 1) * LANES)
            hn = ((x_ref[rows, cols] * rs * gain_ref[:, cols]) * (1.0 + sc_ref[:, cols])
                  + sh_ref[:, cols])
            h_ref[rows, cols] = hn.astype(BF16)
        return carry

    lax.fori_loop(0, x_ref.shape[0] // ROW_CHUNK, body, 0)


def _side_cast_specs(side, n_row_steps, n_cols_steps):
    in_specs, out_specs, shapes = [], [], []
    for w, layer in side:
        _, rows, cols = w.shape
        rb = CAST_ROWS
        while rows // rb > n_row_steps * n_cols_steps:
            rb *= 2
        assert rows % rb == 0
        block = lambda i, j, last=rows // rb - 1: jnp.minimum(i * n_cols_steps + j, last)
        in_specs.append(pl.BlockSpec((None, rb, cols),
                                     lambda i, j, b=block, l=layer: (l, b(i, j), 0)))
        out_specs.append(pl.BlockSpec((rb, cols), lambda i, j, b=block: (b(i, j), 0)))
        shapes.append(jax.ShapeDtypeStruct((rows, cols), BF16))
    return in_specs, out_specs, shapes


def _side_cast(src_refs, dst_refs):
    for src, dst in zip(src_refs, dst_refs):
        dst[...] = src[...].astype(BF16)


def _x_tile_spec(tm, d):
    return pl.BlockSpec((tm, d), lambda i, j: (i, 0), pipeline_mode=pl.Buffered(1))


def _mod_spec(d, tiles_per_batch, row, chunk):
    return pl.BlockSpec((None, 1, d), lambda i, j: (row + i // tiles_per_batch, 0, chunk))


def _att_prologue(x_ref, gain_ref, sc_ref, sh_ref, h_ref, rs_ref, stage_ref, stage2_ref):
    tm, d = x_ref.shape
    n_chunks = tm // ROW_CHUNK

    _row_rsqrt_to_scratch(x_ref, rs_ref)

    for cg in range(d // (STAGE_COLS * LANES)):
        col = lambda cc: slice((cg * STAGE_COLS + cc) * LANES, (cg * STAGE_COLS + cc + 1) * LANES)

        def natural(i, carry):
            rows = _row_chunk(i, ROW_CHUNK)
            rs = rs_ref[rows, :]
            for cc in range(STAGE_COLS):
                cols = col(cc)
                hn = ((x_ref[rows, cols] * rs * gain_ref[:, cols]) * (1.0 + sc_ref[:, cols])
                      + sh_ref[:, cols])
                h_ref[0, rows, cols] = hn.astype(BF16)
                stage_ref[cc, rows, :] = hn
            return carry

        lax.fori_loop(0, n_chunks, natural, 0)

        dil = DILATED_PATTERNS[1][1]
        per = tm // dil
        sub = DILATED_PATTERNS[2][1] // dil
        per16 = per // sub

        def permute4(r, carry):
            dst = _row_chunk(r, per)
            for cc in range(STAGE_COLS):
                v = stage_ref[cc, pl.ds(r, per, stride=dil), :]
                h_ref[1, dst, col(cc)] = v.astype(BF16)
                stage2_ref[cc, dst, :] = v
            return carry

        lax.fori_loop(0, dil, permute4, 0)

        def permute16(r, carry):
            src = pl.ds((r % dil) * per + r // dil, per16, stride=sub)
            dst = _row_chunk(r, per16)
            for cc in range(STAGE_COLS):
                h_ref[2, dst, col(cc)] = stage2_ref[cc, src, :].astype(BF16)
            return carry

        lax.fori_loop(0, dil * sub, permute16, 0)


def _att_proj_kernel(x_ref, gain_ref, sc_ref, sh_ref, w_ref, hg_ref, *rest, d_model, n_side):
    side_in, rest = rest[:n_side], rest[n_side:]
    o_ref, side_out = rest[0], rest[1:1 + n_side]
    h_ref, rs_ref, stage_ref, stage2_ref = rest[1 + n_side:]
    j = pl.program_id(1)

    @pl.when(j == 0)
    def _():
        _att_prologue(x_ref, gain_ref, sc_ref, sh_ref, h_ref, rs_ref, stage_ref, stage2_ref)

    tn = w_ref.shape[1]
    group = (j * tn) // (3 * d_model)
    is_v = ((j * tn) // d_model) % 3 == 2

    def tile(head_norm):
        _side_cast(side_in, side_out)
        for rc in range(h_ref.shape[1] // MM_ROWS):
            rows = slice(rc * MM_ROWS, (rc + 1) * MM_ROWS)
            acc = jnp.dot(h_ref[group, rows, :], w_ref[...], preferred_element_type=F32)
            if not head_norm:
                o_ref[rows, :] = acc.astype(BF16)
                continue
            for hh in range(tn // HEAD_DIM):
                sl = slice(hh * HEAD_DIM, (hh + 1) * HEAD_DIM)
                a = acc[:, sl]
                ms = jnp.mean(a * a, axis=-1, keepdims=True)
                o_ref[rows, sl] = (a * lax.rsqrt(ms + NORM_EPS) * hg_ref[:, sl]).astype(BF16)

    pl.when(is_v)(functools.partial(tile, False))
    pl.when(jnp.logical_not(is_v))(functools.partial(tile, True))


def _att_proj(x, gain, mod3, mod_row, w, head_gain, tiles_per_batch, side):
    t, d = x.shape
    n = w.shape[1]
    side_in, side_out, side_shapes = _side_cast_specs(side, t // TM, n // TN_WIDE)
    outs = pl.pallas_call(
        functools.partial(_att_proj_kernel, d_model=d, n_side=len(side)),
        grid=(t // TM, n // TN_WIDE),
        in_specs=[_x_tile_spec(TM, d),
                  pl.BlockSpec((1, d), lambda i, j: (0, 0)),
                  _mod_spec(d, tiles_per_batch, mod_row, 1),
                  _mod_spec(d, tiles_per_batch, mod_row, 0),
                  pl.BlockSpec((d, TN_WIDE), lambda i, j: (0, j)),
                  pl.BlockSpec((1, TN_WIDE), lambda i, j: (0, j))] + side_in,
        out_specs=[pl.BlockSpec((TM, TN_WIDE), lambda i, j: (i, j))] + side_out,
        out_shape=[jax.ShapeDtypeStruct((t, n), BF16)] + side_shapes,
        scratch_shapes=[pltpu.VMEM((N_GROUPS, TM, d), BF16),
                        pltpu.VMEM((TM, LANES), F32),
                        pltpu.VMEM((STAGE_COLS, TM, LANES), F32),
                        pltpu.VMEM((STAGE_COLS, TM, LANES), F32)],
        compiler_params=_params(2),
        name="att_qkv_proj",
    )(x, gain, mod3, mod3, w, head_gain, *[w for w, _ in side])
    return outs[0], outs[1:]


def _proj_kernel(x_ref, gain_ref, sc_ref, sh_ref, cos_ref, sin_ref, w_ref, *rest, rot_cols,
                 gate_col0, n_side):
    side_in, rest = rest[:n_side], rest[n_side:]
    o_ref, side_out = rest[0], rest[1:1 + n_side]
    h_ref, rs_ref = rest[1 + n_side:]
    j = pl.program_id(1)

    @pl.when(j == 0)
    def _():
        _norm_mod_to_scratch(x_ref, gain_ref, sc_ref, sh_ref, h_ref, rs_ref)

    tn = w_ref.shape[1]
    half = RET_HEAD_QK // 2

    def tile(kind):
        _side_cast(side_in, side_out)
        for rc in range(h_ref.shape[0] // MM_ROWS):
            rows = slice(rc * MM_ROWS, (rc + 1) * MM_ROWS)
            acc = jnp.dot(h_ref[rows, :], w_ref[...], preferred_element_type=F32)
            if kind == "rotary":
                cos, sin = cos_ref[rows, :], sin_ref[rows, :]
                for hh in range(tn // RET_HEAD_QK):
                    lo = slice(hh * RET_HEAD_QK, hh * RET_HEAD_QK + half)
                    hi = slice(hh * RET_HEAD_QK + half, (hh + 1) * RET_HEAD_QK)
                    o_ref[rows, lo] = (acc[:, lo] * cos - acc[:, hi] * sin).astype(BF16)
                    o_ref[rows, hi] = (acc[:, lo] * sin + acc[:, hi] * cos).astype(BF16)
            else:
                o_ref[rows, :] = (_silu(acc) if kind == "silu" else acc).astype(BF16)

    col0 = j * tn
    pl.when(col0 < rot_cols)(functools.partial(tile, "rotary"))
    pl.when(jnp.logical_and(col0 >= rot_cols, col0 < gate_col0))(functools.partial(tile, "plain"))
    pl.when(col0 >= gate_col0)(functools.partial(tile, "silu"))


def _proj(x, gain, mod3, mod_row, w, seq, cos, sin, rot_cols, gate_col0, side):
    t, d = x.shape
    n = w.shape[1]
    tm = TM
    tiles_per_batch = seq // tm
    tn = TN_WIDE
    assert rot_cols % tn == 0 and gate_col0 % tn == 0
    side_in, side_out, side_shapes = _side_cast_specs(side, t // tm, n // tn)
    pos_spec = pl.BlockSpec((tm, cos.shape[1]), lambda i, j: (i % tiles_per_batch, 0))
    outs = pl.pallas_call(
        functools.partial(_proj_kernel, rot_cols=rot_cols, gate_col0=gate_col0,
                          n_side=len(side)),
        grid=(t // tm, n // tn),
        in_specs=[pl.BlockSpec((tm, d), lambda i, j: (i, 0)),
                  pl.BlockSpec((1, d), lambda i, j: (0, 0)),
                  _mod_spec(d, tiles_per_batch, mod_row, 1),
                  _mod_spec(d, tiles_per_batch, mod_row, 0),
                  pos_spec, pos_spec,
                  pl.BlockSpec((d, tn), lambda i, j: (0, j))] + side_in,
        out_specs=[pl.BlockSpec((tm, tn), lambda i, j: (i, j))] + side_out,
        out_shape=[jax.ShapeDtypeStruct((t, n), BF16)] + side_shapes,
        scratch_shapes=[pltpu.VMEM((tm, d), BF16), pltpu.VMEM((tm, LANES), F32)],
        compiler_params=_params(2),
        name="ret_qkvg_proj",
    )(x, gain, mod3, mod3, cos, sin, w, *[w for w, _ in side])
    return outs[0], outs[1:]


def _ffn_up_kernel(x_ref, gain_ref, sc_ref, sh_ref, wg_ref, wu_ref, *rest, n_side):
    side_in, rest = rest[:n_side], rest[n_side:]
    o_ref, side_out = rest[0], rest[1:1 + n_side]
    h_ref, rs_ref = rest[1 + n_side:]

    @pl.when(pl.program_id(1) == 0)
    def _():
        _norm_mod_to_scratch(x_ref, gain_ref, sc_ref, sh_ref, h_ref, rs_ref)

    _side_cast(side_in, side_out)
    for rc in range(h_ref.shape[0] // MM_ROWS):
        rows = slice(rc * MM_ROWS, (rc + 1) * MM_ROWS)
        h = h_ref[rows, :]
        gate = jnp.dot(h, wg_ref[...], preferred_element_type=F32)
        up = jnp.dot(h, wu_ref[...], preferred_element_type=F32)
        o_ref[rows, :] = (_silu(gate) * up).astype(BF16)


def _ffn_up(x, gain, mod3, mod_row, w_up, seq, side):
    t, d = x.shape
    d_ff = w_up.shape[1] // 2
    tf = 512
    n_f = d_ff // tf
    tm = TM
    tiles_per_batch = seq // tm
    side_in, side_out, side_shapes = _side_cast_specs(side, t // tm, n_f)
    outs = pl.pallas_call(
        functools.partial(_ffn_up_kernel, n_side=len(side)),
        grid=(t // tm, n_f),
        in_specs=[pl.BlockSpec((tm, d), lambda i, j: (i, 0)),
                  pl.BlockSpec((1, d), lambda i, j: (0, 0)),
                  _mod_spec(d, tiles_per_batch, mod_row, 4),
                  _mod_spec(d, tiles_per_batch, mod_row, 3),
                  pl.BlockSpec((d, tf), lambda i, j: (0, j)),
                  pl.BlockSpec((d, tf), lambda i, j: (0, j + n_f))] + side_in,
        out_specs=[pl.BlockSpec((tm, tf), lambda i, j: (i, j))] + side_out,
        out_shape=[jax.ShapeDtypeStruct((t, d_ff), BF16)] + side_shapes,
        scratch_shapes=[pltpu.VMEM((tm, d), BF16), pltpu.VMEM((tm, LANES), F32)],
        compiler_params=_params(2),
        name="ffn_up",
    )(x, gain, mod3, mod3, w_up, w_up, *[w for w, _ in side])
    return outs[0], outs[1:]


def _out_proj_kernel(a_ref, w_ref, x_ref, g_ref, *rest, n_side):
    side_in, o_ref, side_out = rest[:n_side], rest[n_side], rest[n_side + 1:]
    _side_cast(side_in, side_out)
    for rc in range(a_ref.shape[0] // MM_ROWS):
        rows = slice(rc * MM_ROWS, (rc + 1) * MM_ROWS)
        y = jnp.dot(a_ref[rows, :], w_ref[...], preferred_element_type=F32)
        o_ref[rows, :] = x_ref[rows, :] + g_ref[...] * y


def _gate_spec(n, tn, mod_row, gate_chunk, tiles_per_batch):
    return pl.BlockSpec((None, 1, tn),
                        lambda i, j: (mod_row + i // tiles_per_batch, 0,
                                      gate_chunk * (n // tn) + j))


def _out_proj(a, w, x, mod3, mod_row, gate_chunk, tiles_per_batch, tm, tn, name, side=()):
    t, k = a.shape
    n = w.shape[1]
    side_in, side_out, side_shapes = _side_cast_specs(side, t // tm, n // tn)
    w_mode = dict(pipeline_mode=pl.Buffered(1)) if tn == n else {}
    outs = pl.pallas_call(
        functools.partial(_out_proj_kernel, n_side=len(side)),
        grid=(t // tm, n // tn),
        in_specs=[pl.BlockSpec((tm, k), lambda i, j: (i, 0)),
                  pl.BlockSpec((k, tn), lambda i, j: (0, j), **w_mode),
                  pl.BlockSpec((tm, tn), lambda i, j: (i, j)),
                  _gate_spec(n, tn, mod_row, gate_chunk, tiles_per_batch)] + side_in,
        out_specs=[pl.BlockSpec((tm, tn), lambda i, j: (i, j))] + side_out,
        out_shape=[jax.ShapeDtypeStruct((t, n), F32)] + side_shapes,
        compiler_params=_params(2),
        name=name,
    )(a, w, x, mod3, *[w for w, _ in side])
    return outs[0], outs[1:]


def _bias_kernel(tab_ref, bucket_ref, valid_ref, o_ref):
    g = pl.program_id(0)
    h = pl.program_id(1)
    bucket = bucket_ref[...]
    col = g * ATT_HEADS + h
    acc = jnp.zeros(bucket.shape, F32)
    for b in range(REL_BUCKETS):
        acc = jnp.where(bucket == b, tab_ref[b, col] * LOG2E, acc)
    for first in range(2):
        o_ref[first] = jnp.where(valid_ref[first] != 0, acc, MASK_VALUE)


def _t5_bucket_np(dist):
    max_exact = REL_BUCKETS // 2
    d_f = np.maximum(dist, 1).astype(np.float32)
    large = max_exact + (np.log(d_f / np.float32(max_exact))
                         / np.float32(math.log(REL_MAX_DIST / max_exact))
                         * np.float32(REL_BUCKETS - max_exact)).astype(np.int32)
    large = np.minimum(large, REL_BUCKETS - 1)
    return np.where(dist < max_exact, dist, large).astype(np.int32)


def _bias_tiles(rel_bias):
    blk = ATT_BLK
    qi = np.arange(blk)[:, None]
    ki = np.arange(2 * blk)[None, :]
    delta = blk + qi - ki
    band = (delta >= 0) & (delta <= blk)
    valid = np.stack([band & (ki >= blk), band]).astype(np.int32)
    buckets = np.stack([_t5_bucket_np(np.maximum(delta, 0) * dil)
                        for _, dil in DILATED_PATTERNS])
    return pl.pallas_call(
        _bias_kernel,
        grid=(N_GROUPS, ATT_HEADS),
        in_specs=[pl.BlockSpec(memory_space=pltpu.SMEM),
                  pl.BlockSpec((None, blk, 2 * blk), lambda g, h: (g, 0, 0)),
                  pl.BlockSpec((2, blk, 2 * blk), lambda g, h: (0, 0, 0))],
        out_specs=pl.BlockSpec((None, 2, None, blk, 2 * blk), lambda g, h: (g, 0, h, 0, 0)),
        out_shape=jax.ShapeDtypeStruct((N_GROUPS, 2, ATT_HEADS, blk, 2 * blk), F32),
        compiler_params=_params(2),
        name="rel_bias_tiles",
    )(rel_bias, jnp.asarray(buckets), jnp.asarray(valid))


ATT_PASS = 8


def _att_units(q_ref, kc_ref, kp_ref, vc_ref, vp_ref, bias_ref, store, *, g, heads, first_span):
    blk = ATT_BLK
    dil = DILATED_PATTERNS[g][1]
    dn = (((1,), (1,)), ((), ()))
    ones = jnp.ones((2 * blk, HEAD_DIM), BF16)
    lane = lax.broadcasted_iota(jnp.int32, (blk, LANES), 1)
    head0 = pl.program_id(1) * heads

    def group(units):
        sl = lambda h: slice(h * HEAD_DIM, (h + 1) * HEAD_DIM)
        pairs = [(ui, h) for ui in range(len(units)) for h in range(heads)]
        m_tiles = [jnp.zeros((blk, LANES), F32) for _ in units]
        l_tiles = [jnp.ones((blk, LANES), F32) for _ in units]
        for p0 in range(0, len(pairs), ATT_PASS):
            batch = pairs[p0:p0 + ATT_PASS]
            pieces = []
            for ui, h in batch:
                _, q_of, k_of, _, variant = units[ui]
                pieces.append(lax.dot_general(q_of(sl(h)), k_of(sl(h)), dn,
                                              preferred_element_type=F32)
                              + bias_ref[variant, h])
            s = jnp.concatenate(pieces, axis=0)
            m = jnp.max(jnp.maximum(s[:, :blk], s[:, blk:]), axis=-1, keepdims=True)
            p = jnp.exp2(s - m).astype(BF16)
            for idx, (ui, h) in enumerate(batch):
                u, _, _, v_of, _ = units[ui]
                rows = slice(idx * blk, (idx + 1) * blk)
                v_ext = jnp.concatenate([v_of(sl(h)), ones], axis=1)
                pv = jnp.dot(p[rows], v_ext, preferred_element_type=F32)
                denom = pv[:, HEAD_DIM:]
                o = pv[:, :HEAD_DIM] / denom
                mine = lane == head0 + h
                m_tiles[ui] = jnp.where(mine, m[rows], m_tiles[ui])
                l_tiles[ui] = jnp.where(mine, denom, l_tiles[ui])
                store(u, h, o, None)
        for ui, unit_args in enumerate(units):
            store(unit_args[0], None, None, m_tiles[ui] + jnp.log2(l_tiles[ui]))

    unit = lambda *unit_args: group([unit_args])

    if dil == 1:
        n_units = q_ref.shape[0] // blk
        unit(0,
             lambda sl: q_ref[0:blk, sl],
             lambda sl: jnp.concatenate([kp_ref[:, sl], kc_ref[0:blk, sl]], axis=0),
             lambda sl: jnp.concatenate([vp_ref[:, sl], vc_ref[0:blk, sl]], axis=0),
             jnp.where(first_span, 0, 1))

        def body(u, carry):
            cur = _row_chunk(u, blk)
            both = pl.ds(pl.multiple_of((u - 1) * blk, blk), 2 * blk)
            unit(u, lambda sl: q_ref[cur, sl], lambda sl: kc_ref[both, sl],
                 lambda sl: vc_ref[both, sl], 1)
            return carry

        lax.fori_loop(1, n_units, body, 0)
    else:
        variant = jnp.where(first_span, 0, 1)
        if dil == 4:
            pick = lambda ref, r, sl: ref[r, :, sl]
            prev_cur = lambda pref, cref, r, sl: jnp.concatenate(
                [pref[r, :, sl], cref[r, :, sl]], axis=0)
        else:
            pick = lambda ref, r, sl: jnp.concatenate([ref[0, r, :, sl], ref[1, r, :, sl]], axis=0)
            prev_cur = lambda pref, cref, r, sl: jnp.concatenate(
                [pref[0, r, :, sl], pref[1, r, :, sl], cref[0, r, :, sl], cref[1, r, :, sl]],
                axis=0)

        per_iter = ATT_HEADS // heads

        def body(it, carry):
            def unit_args(r):
                return (r, lambda sl: pick(q_ref, r, sl),
                        lambda sl: prev_cur(kp_ref, kc_ref, r, sl),
                        lambda sl: prev_cur(vp_ref, vc_ref, r, sl), variant)

            group([unit_args(it * per_iter + sub) for sub in range(per_iter)])
            return carry

        lax.fori_loop(0, dil // per_iter, body, 0)


def _att_kernel(q_ref, kc_ref, vc_ref, bias_ref, o_ref, lse_ref, kp_ref, vp_ref, *scratch,
                g, heads):
    blk = ATT_BLK
    dil = DILATED_PATTERNS[g][1]
    first_span = pl.program_id(2) == 0

    @pl.when(first_span)
    def _():
        kp_ref[...] = jnp.zeros_like(kp_ref)
        vp_ref[...] = jnp.zeros_like(vp_ref)

    if dil == 1:
        def store(u, h, o, lse_tile):
            rows = _row_chunk(u, blk)
            if h is None:
                lse_ref[rows, :] = lse_tile
            else:
                o_ref[rows, h * HEAD_DIM:(h + 1) * HEAD_DIM] = o.astype(BF16)
    else:
        o_scr, lse_scr = scratch

        def store(u, h, o, lse_tile):
            rows = pl.ds(u, blk, stride=dil)
            if h is None:
                lse_scr[rows, :] = lse_tile
            else:
                o_scr[h, rows, :] = o

    _att_units(q_ref, kc_ref, kp_ref, vc_ref, vp_ref, bias_ref, store, g=g, heads=heads,
               first_span=first_span)

    if dil == 1:
        last = slice(kc_ref.shape[0] - blk, kc_ref.shape[0])
        kp_ref[...] = kc_ref[last, :]
        vp_ref[...] = vc_ref[last, :]
    else:
        kp_ref[...] = kc_ref[...]
        vp_ref[...] = vc_ref[...]
        for h in range(heads):
            o_ref[:, h * HEAD_DIM:(h + 1) * HEAD_DIM] = o_scr[h].astype(BF16)
        lse_ref[...] = lse_scr[...]


def _attention_group(qkv, bias, g, batch, seq):
    dil = DILATED_PATTERNS[g][1]
    blk = ATT_BLK
    span = dil * blk if dil > 1 else 8 * blk
    heads = ATT_HEADS if dil < 16 else 4
    n_hg = ATT_HEADS // heads
    width = heads * HEAD_DIM
    n_cols = qkv.shape[1]
    col0 = lambda which: (g * 3 + which) * (ATT_HEADS // heads)

    if dil == 1:
        view = qkv.reshape(batch, seq, n_cols)
        cur = lambda which: pl.BlockSpec((None, span, width),
                                         lambda b, h, s: (b, s, col0(which) + h))
        prev_shape = (blk, width)
        scratch = []
    elif dil == 4:
        halves = TM // (dil * blk)
        view = qkv.reshape(batch, seq // TM, dil, halves, blk, n_cols)
        cur = lambda which: pl.BlockSpec(
            (None, None, dil, None, blk, width),
            lambda b, h, s: (b, s // halves, 0, s % halves, 0, col0(which) + h))
        prev_shape = (dil, blk, width)
        scratch = [pltpu.VMEM((heads, span, HEAD_DIM), F32), pltpu.VMEM((span, LANES), F32)]
    else:
        tiles = span // TM
        per = TM // dil
        view = qkv.reshape(batch, seq // span, tiles, dil, per, n_cols)
        cur = lambda which: pl.BlockSpec((None, None, tiles, dil, per, width),
                                         lambda b, h, s: (b, s, 0, 0, 0, col0(which) + h))
        prev_shape = (tiles, dil, per, width)
        scratch = [pltpu.VMEM((heads, span, HEAD_DIM), F32), pltpu.VMEM((span, LANES), F32)]

    o, lse = pl.pallas_call(
        functools.partial(_att_kernel, g=g, heads=heads),
        grid=(batch, n_hg, seq // span),
        in_specs=[cur(0), cur(1), cur(2),
                  pl.BlockSpec((None, 2, heads, blk, 2 * blk), lambda b, h, s: (g, 0, h, 0, 0))],
        out_specs=[pl.BlockSpec((None, span, width), lambda b, h, s: (b, s, h)),
                   pl.BlockSpec((None, None, span, LANES), lambda b, h, s: (b, h, s, 0))],
        out_shape=[jax.ShapeDtypeStruct((batch, seq, ATT_HEADS * HEAD_DIM), BF16),
                   jax.ShapeDtypeStruct((batch, n_hg, seq, LANES), F32)],
        scratch_shapes=[pltpu.VMEM(prev_shape, BF16), pltpu.VMEM(prev_shape, BF16)] + scratch,
        compiler_params=_params(3),
        name=f"dilated_attention_g{g}",
    )(view, view, view, bias)
    return o, lse


MERGE_ROWS = 256


def _att_out_kernel(o0_ref, o1_ref, o2_ref, l0_ref, l1_ref, l2_ref, w_ref, x_ref, g_ref,
                    out_ref):
    o_refs = (o0_ref, o1_ref, o2_ref)
    l_refs = (l0_ref, l1_ref, l2_ref)
    tm = x_ref.shape[0]
    lane = lax.broadcasted_iota(jnp.int32, (MERGE_ROWS, LANES), 1)
    for rc in range(tm // MERGE_ROWS):
        rows = slice(rc * MERGE_ROWS, (rc + 1) * MERGE_ROWS)
        lses = []
        for l_ref in l_refs:
            tile = l_ref[0, rows, :]
            for hg in range(1, l_ref.shape[0]):
                tile = tile + l_ref[hg, rows, :]
            lses.append(tile)
        mx = jnp.maximum(jnp.maximum(lses[0], lses[1]), lses[2])
        es = [jnp.exp2(t - mx) for t in lses]
        den = es[0] + es[1] + es[2]
        ws = [es[0] / den, es[1] / den]
        pieces = []
        for h in range(ATT_HEADS):
            sl = slice(h * HEAD_DIM, (h + 1) * HEAD_DIM)
            base = o_refs[2][rows, sl].astype(F32)
            acc = base
            for w_g, o_ref in zip(ws, o_refs[:2]):
                w_col = jnp.sum(jnp.where(lane == h, w_g, 0.0), axis=-1, keepdims=True)
                acc = acc + w_col * (o_ref[rows, sl].astype(F32) - base)
            pieces.append(acc.astype(BF16))
        merged = jnp.concatenate(pieces, axis=1)
        y = jnp.dot(merged, w_ref[...], preferred_element_type=F32)
        out_ref[rows, :] = x_ref[rows, :] + g_ref[...] * y


def _att_out_proj(outs, lses, w, x, mod3, mod_row, tiles_per_batch_tm, tm):
    t, n = x.shape
    k = w.shape[0]
    seq_tiles = tiles_per_batch_tm
    o_spec = pl.BlockSpec((tm, k), lambda i, j: (i, 0))
    l_spec = lambda n_hg: pl.BlockSpec((None, n_hg, tm, LANES),
                                       lambda i, j: (i // seq_tiles, 0, i % seq_tiles, 0))
    return pl.pallas_call(
        _att_out_kernel,
        grid=(t // tm, 1),
        in_specs=[o_spec, o_spec, o_spec] + [l_spec(l.shape[1]) for l in lses]
                 + [pl.BlockSpec((k, n), lambda i, j: (0, 0)),
                    pl.BlockSpec((tm, n), lambda i, j: (i, 0)),
                    _gate_spec(n, n, mod_row, 2, seq_tiles)],
        out_specs=pl.BlockSpec((tm, n), lambda i, j: (i, 0)),
        out_shape=jax.ShapeDtypeStruct((t, n), F32),
        compiler_params=_params(2),
        name="att_merge_out_proj",
    )(*[o.reshape(t, k) for o in outs], *lses, w, x, mod3)


RET_TOKENS = 2048
RET_HEADS_PER_STEP = 2


def _retention_kernel(q_ref, k_ref, v_ref, g_ref, inner_ref, cross_ref, sdec_ref, cdec_ref,
                      o_ref, state_ref):
    @pl.when(pl.program_id(2) == 0)
    def _():
        state_ref[...] = jnp.zeros_like(state_ref)

    dn_nt = (((1,), (1,)), ((), ()))
    dn_tn = (((0,), (0,)), ((), ()))
    for c in range(RET_TOKENS // RET_CHUNK):
        rows = slice(c * RET_CHUNK, (c + 1) * RET_CHUNK)
        for hh in range(RET_HEADS_PER_STEP):
            qk_cols = slice(hh * RET_HEAD_QK, (hh + 1) * RET_HEAD_QK)
            v_cols = slice(hh * RET_HEAD_V, (hh + 1) * RET_HEAD_V)
            q_b = q_ref[rows, qk_cols]
            k_b = k_ref[rows, qk_cols]
            qc_b = (q_b.astype(F32) * cross_ref[hh]).astype(BF16)
            ks_b = (k_b.astype(F32) * sdec_ref[hh]).astype(BF16)
            v = v_ref[rows, v_cols]

            scores = (lax.dot_general(q_b, k_b, dn_nt, preferred_element_type=F32)
                      * inner_ref[hh])
            state = state_ref[hh]
            o = (jnp.dot(scores.astype(BF16), v, preferred_element_type=F32)
                 + jnp.dot(qc_b, state.astype(BF16), preferred_element_type=F32))
            state_ref[hh] = state * cdec_ref[hh] + lax.dot_general(
                ks_b, v, dn_tn, preferred_element_type=F32)

            ms = jnp.mean(o * o, axis=-1, keepdims=True)
            gate = g_ref[rows, v_cols].astype(F32)
            o_ref[rows, v_cols] = (gate * (o * lax.rsqrt(ms + NORM_EPS))).astype(BF16)


def _rotary_tables(seq):
    half = RET_HEAD_QK // 2
    inv = 1.0 / (10000.0 ** jnp.linspace(0.0, 1.0, half, dtype=F32))
    ang = jnp.arange(seq, dtype=F32)[:, None] * inv[None, :]
    return jnp.cos(ang), jnp.sin(ang)


def _retention(qkvg, batch, seq):
    c = RET_CHUNK
    hq, hv = RET_HEAD_QK, RET_HEAD_V
    hps = RET_HEADS_PER_STEP
    n_qk = RET_HEADS * hq
    n_v = RET_HEADS * hv

    log_gamma = jnp.log(1.0 - 2.0 ** (-5.0 - jnp.arange(RET_HEADS, dtype=F32)))
    pos = jnp.arange(c, dtype=F32)
    diff = pos[:, None] - pos[None, :]
    inner = jnp.where(diff[None] >= 0,
                      jnp.exp(jnp.maximum(diff, 0.0)[None] * log_gamma[:, None, None]), 0.0)
    cross = jnp.exp((pos[None, :] + 1.0) * log_gamma[:, None])
    sdec = jnp.exp((c - 1.0 - pos)[None, :] * log_gamma[:, None])
    cdec = jnp.exp(c * log_gamma)
    k_scale = hq ** -0.5
    inner = inner * k_scale
    cross_l = jnp.broadcast_to(cross[:, :, None], (RET_HEADS, c, hq))
    sdec_l = jnp.broadcast_to((sdec * k_scale)[:, :, None], (RET_HEADS, c, hq))
    cdec_l = jnp.broadcast_to(cdec[:, None, None], (RET_HEADS, 1, hv))

    x = qkvg.reshape(batch, seq, qkvg.shape[1])
    tt = RET_TOKENS
    wq, wv = hps * hq, hps * hv
    head_tab = lambda shape: pl.BlockSpec((hps,) + shape, lambda b, h, s: (h, 0, 0))
    out = pl.pallas_call(
        _retention_kernel,
        grid=(batch, RET_HEADS // hps, seq // tt),
        in_specs=[pl.BlockSpec((None, tt, wq), lambda b, h, s: (b, s, h)),
                  pl.BlockSpec((None, tt, wq), lambda b, h, s: (b, s, n_qk // wq + h)),
                  pl.BlockSpec((None, tt, wv), lambda b, h, s: (b, s, 2 * n_qk // wv + h)),
                  pl.BlockSpec((None, tt, wv), lambda b, h, s: (b, s, (2 * n_qk + n_v) // wv + h)),
                  head_tab((c, c)), head_tab((c, hq)), head_tab((c, hq)), head_tab((1, hv))],
        out_specs=pl.BlockSpec((None, tt, wv), lambda b, h, s: (b, s, h)),
        out_shape=jax.ShapeDtypeStruct((batch, seq, n_v), BF16),
        scratch_shapes=[pltpu.VMEM((hps, hq, hv), F32)],
        compiler_params=_params(3),
        name="retention",
    )(x, x, x, x, inner, cross_l, sdec_l, cdec_l)
    return out.reshape(batch * seq, n_v)


def kernel(x, c, w_mod, b_mod, norm_mix, norm_ffn, rel_bias, att_w_qkv, att_q_gain, att_k_gain,
           att_w_o, ret_w_qkvg, ret_w_o, ffn_w_up, ffn_w_down):
    batch, seq, d = x.shape
    depth = w_mod.shape[0]
    assert seq % (2 * TM) == 0 and d == ATT_HEADS * HEAD_DIM
    tiles_per_batch = seq // TM

    mod = _modulation(c, w_mod, b_mod)
    mod3 = mod.reshape(depth * batch, 1, 6 * d)
    bias = _bias_tiles(rel_bias)
    xf = x.reshape(batch * seq, d)

    ready = {}

    def bf16_weight(name, layer, w):
        return ready.pop((name, layer)) if (name, layer) in ready else w.astype(BF16)

    def first_weight(layer):
        if layer >= depth:
            return None
        return (("att_w_qkv", (att_w_qkv, layer // 2)) if layer % 2 == 0
                else ("ret_w_qkvg", (ret_w_qkvg, layer // 2)))

    for i in range(depth):
        jdx = i // 2
        mod_row = i * batch
        gain_mix = norm_mix[i].reshape(1, d)
        if i % 2 == 0:
            scale = HEAD_DIM ** -0.5 * LOG2E
            ones = jnp.ones((ATT_HEADS * HEAD_DIM,), F32)
            head_gain = jnp.concatenate(
                [jnp.concatenate([jnp.tile(att_q_gain[jdx, g] * scale, ATT_HEADS),
                                  jnp.tile(att_k_gain[jdx, g], ATT_HEADS), ones])
                 for g in range(N_GROUPS)]).reshape(1, -1)
            qkv, (w_o, w_up) = _att_proj(xf, gain_mix, mod3, mod_row,
                                         bf16_weight("att_w_qkv", i, att_w_qkv[jdx]), head_gain,
                                         tiles_per_batch, [(att_w_o, jdx), (ffn_w_up, i)])
            ready[("ffn_w_up", i)] = w_up
            outs, lses = zip(*[_attention_group(qkv, bias, g, batch, seq)
                               for g in range(N_GROUPS)])
            att_tm = 512
            xf = _att_out_proj(outs, lses, w_o, xf, mod3, mod_row, seq // att_tm, att_tm)
        else:
            n_qk = RET_HEADS * RET_HEAD_QK
            qkvg, (w_o, w_up) = _proj(xf, gain_mix, mod3, mod_row,
                                      bf16_weight("ret_w_qkvg", i, ret_w_qkvg[jdx]), seq,
                                      *_rotary_tables(seq), 2 * n_qk,
                                      2 * n_qk + RET_HEADS * RET_HEAD_V,
                                      [(ret_w_o, jdx), (ffn_w_up, i)])
            ready[("ffn_w_up", i)] = w_up
            mixed = _retention(qkvg, batch, seq)
            xf, _ = _out_proj(mixed, w_o, xf, mod3, mod_row, 2, tiles_per_batch, TM, TN,
                              "ret_out_proj")

        nxt = first_weight(i + 1)
        act, cast = _ffn_up(xf, norm_ffn[i].reshape(1, d), mod3, mod_row,
                            bf16_weight("ffn_w_up", i, ffn_w_up[i]), seq,
                            [(ffn_w_down, i)] + ([nxt[1]] if nxt else []))
        if nxt:
            ready[(nxt[0], i + 1)] = cast[1]
        xf, _ = _out_proj(act, cast[0], xf, mod3, mod_row, 5, seq // FFN_DOWN_TM, FFN_DOWN_TM, d,
                          "ffn_down")
    return xf.reshape(batch, seq, d)
```
